```python
import math, functools
import jax, jax.numpy as jnp
from jax import lax
import numpy as np

D_MODEL = 1024
BATCH = 32
SEQ = 2048
DEPTH = 1
DEC_BATCH = 128
DEC_SEQ = 8
PAST_LEN = 8192
PAGE_SIZE = 128

SSM_EXPAND = 2
D_INNER = SSM_EXPAND * D_MODEL
SSM_HEAD_DIM = 64
N_SSM_HEADS = D_INNER // SSM_HEAD_DIM
N_SSM_GROUPS = 4
D_STATE = 128
D_CONV = 4
SSD_CHUNK = 128
CONV_DIM = D_INNER + 2 * N_SSM_GROUPS * D_STATE
ATT_HEAD_DIM = 64
N_ATT_HEADS = D_MODEL // (2 * ATT_HEAD_DIM)
N_KV_HEADS = 4
Q_PER_KV = N_ATT_HEADS // N_KV_HEADS
ATT_WIDTH = N_ATT_HEADS * 2 * ATT_HEAD_DIM
KV_WIDTH = N_KV_HEADS * 2 * ATT_HEAD_DIM
Q_BLOCK = 128
D_FF = -(-(8 * D_MODEL) // (3 * 256)) * 256
NORM_EPS = 1e-6
SUBLN_EPS = 1e-5
PROJ_SPLITS = (D_INNER, CONV_DIM, N_SSM_HEADS, ATT_WIDTH, KV_WIDTH, KV_WIDTH, 2 * D_MODEL)
IN_WIDTH = sum(PROJ_SPLITS)

kernel_name = "hybrid_ssd_diffattn_decoder_step"


def rms_norm(x, g, eps=NORM_EPS):
    xf = x.astype(jnp.float32)
    y = xf * lax.rsqrt(jnp.mean(jnp.square(xf), axis=-1, keepdims=True) + eps)
    return (y * g.astype(jnp.float32)).astype(x.dtype)


def lambda_init(layer):
    return 0.8 - 0.6 * math.exp(-0.3 * layer)


def split_proj(p):
    cuts = [int(c) for c in np.cumsum(PROJ_SPLITS)[:-1]]
    return jnp.split(p, cuts, axis=-1)


def causal_conv_silu(xbc, prefix, w, b):
    xp = jnp.concatenate([prefix.astype(xbc.dtype), xbc], axis=1)
    L = xbc.shape[1]
    out = b + sum(xp[:, k:k + L] * w[k] for k in range(D_CONV))
    return jax.nn.silu(out), xp[:, xp.shape[1] - (D_CONV - 1):]


def segsum_exp(a):
    cs = jnp.cumsum(a, axis=-1)
    T = a.shape[-1]
    lower = jnp.tril(jnp.ones((T, T), dtype=bool))
    return jnp.exp(jnp.where(lower, cs[..., :, None] - cs[..., None, :], -jnp.inf))


def ssd_scan(x, dt, a_head, bm, cm, h0):
    b, L = x.shape[:2]
    T = min(SSD_CHUNK, L)
    Lp = -(-L // T) * T
    pad = Lp - L
    if pad:
        padl = lambda t: jnp.pad(t, [(0, 0), (0, pad)] + [(0, 0)] * (t.ndim - 2))
        x, dt, bm, cm = padl(x), padl(dt), padl(bm), padl(cm)
    nc = Lp // T
    G, R = N_SSM_GROUPS, N_SSM_HEADS // N_SSM_GROUPS
    xd = (x * dt[..., None]).reshape(b, nc, T, G, R, SSM_HEAD_DIM)
    la = jnp.moveaxis((dt * a_head).reshape(b, nc, T, G, R), 2, -1)
    bc = bm.reshape(b, nc, T, G, D_STATE)
    cc = cm.reshape(b, nc, T, G, D_STATE)
    la_cs = jnp.cumsum(la, axis=-1)
    cb = jnp.einsum("bclgn,bcsgn->bcgls", cc, bc)
    m = cb[:, :, :, None] * segsum_exp(la)
    y_diag = jnp.einsum("bcgrls,bcsgrp->bclgrp", m, xd)
    decay_to_end = jnp.exp(la_cs[..., -1:] - la_cs)
    chunk_states = jnp.einsum("bcsgn,bcgrs,bcsgrp->bcgrpn", bc, decay_to_end, xd)
    states = jnp.concatenate([h0.reshape(b, 1, G, R, SSM_HEAD_DIM, D_STATE), chunk_states], axis=1)
    chunk_decay = jnp.pad(jnp.moveaxis(la_cs[..., -1], 1, -1), [(0, 0), (0, 0), (0, 0), (1, 0)])
    start_states = jnp.einsum("bgrzc,bcgrpn->bzgrpn", segsum_exp(chunk_decay), states)
    y_off = jnp.einsum("bclgn,bcgrpn,bcgrl->bclgrp", cc, start_states[:, :-1], jnp.exp(la_cs))
    y = (y_diag + y_off).reshape(b, Lp, N_SSM_HEADS, SSM_HEAD_DIM)[:, :L]
    h = start_states[:, -1].reshape(b, N_SSM_HEADS, SSM_HEAD_DIM, D_STATE)
    return y, h


def ssm_branch(z, xbc, dt_raw, conv_prefix, h0, lp):
    f32 = jnp.float32
    b, L = z.shape[:2]
    xbc, conv_state = causal_conv_silu(xbc, conv_prefix, lp["conv_w"], lp["conv_b"])
    xs, bm, cm = jnp.split(xbc, [D_INNER, D_INNER + N_SSM_GROUPS * D_STATE], axis=-1)
    xs = xs.reshape(b, L, N_SSM_HEADS, SSM_HEAD_DIM).astype(f32)
    bm = bm.reshape(b, L, N_SSM_GROUPS, D_STATE).astype(f32)
    cm = cm.reshape(b, L, N_SSM_GROUPS, D_STATE).astype(f32)
    dt = jax.nn.softplus(dt_raw.astype(f32) + lp["dt_bias"].astype(f32))
    a_head = -jnp.exp(lp["a_log"].astype(f32))
    y, h = ssd_scan(xs, dt, a_head, bm, cm, h0.astype(f32))
    y = y + xs * lp["d_skip"].astype(f32)[:, None]
    y = y.reshape(b, L, D_INNER) * jax.nn.silu(z.astype(f32))
    y = rms_norm(y.reshape(b, L, N_SSM_GROUPS, D_INNER // N_SSM_GROUPS),
                 lp["ssm_norm_w"].reshape(N_SSM_GROUPS, D_INNER // N_SSM_GROUPS))
    y = y.reshape(b, L, D_INNER).astype(z.dtype)
    return y @ lp["w_ssm_out"], conv_state, h.astype(h0.dtype)


def diff_softmax_attn(q, k, v, q_pos, k_pos, lam):
    s = jnp.einsum("bqkrjd,bskjd->bjkrqs", q, k).astype(jnp.float32) * (ATT_HEAD_DIM ** -0.5)
    causal = k_pos[None, :] <= q_pos[:, None]
    p = jax.nn.softmax(jnp.where(causal, s, -jnp.inf), axis=-1)
    a = p[:, 0] - lam * p[:, 1]
    return jnp.einsum("bkrqs,bskd->bqkrd", a.astype(v.dtype), v)


def attend_prompt(q, k, v, lam):
    b, L = q.shape[:2]
    nb = L // Q_BLOCK
    k_pos = jnp.arange(L)
    qb = jnp.moveaxis(q.reshape((b, nb, Q_BLOCK) + q.shape[2:]), 1, 0)

    def one_block(args):
        qi, i = args
        return diff_softmax_attn(qi, k, v, i * Q_BLOCK + jnp.arange(Q_BLOCK), k_pos, lam)

    o = lax.map(one_block, (qb, jnp.arange(nb)))
    return jnp.moveaxis(o, 0, 1).reshape((b, L) + o.shape[3:])


def attend_paged(q, k, v, lam, k_past, v_past):
    b, L = q.shape[:2]
    past = k_past.shape[1]
    k_all = jnp.concatenate([k_past.reshape(b, past, N_KV_HEADS, 2, ATT_HEAD_DIM).astype(k.dtype), k], axis=1)
    v_all = jnp.concatenate([v_past.astype(v.dtype), v], axis=1)
    return diff_softmax_attn(q, k_all, v_all, past + jnp.arange(L), jnp.arange(past + L), lam)


def trunk_layer(x, conv_prefix, h0, attend, lp, lam_init):
    f32 = jnp.float32
    b, L, _ = x.shape
    h = rms_norm(x, lp["norm_mix_w"])
    z, xbc, dt_raw, q, k, v, gates = split_proj(h @ lp["w_in"])
    y_ssm, conv_state, ssm_state = ssm_branch(z, xbc, dt_raw, conv_prefix, h0, lp)
    lam = (jnp.exp(jnp.sum(lp["lambda_q1"].astype(f32) * lp["lambda_k1"].astype(f32)))
           - jnp.exp(jnp.sum(lp["lambda_q2"].astype(f32) * lp["lambda_k2"].astype(f32))) + lam_init)
    q6 = q.reshape(b, L, N_KV_HEADS, Q_PER_KV, 2, ATT_HEAD_DIM)
    k5 = k.reshape(b, L, N_KV_HEADS, 2, ATT_HEAD_DIM)
    v4 = v.reshape(b, L, N_KV_HEADS, 2 * ATT_HEAD_DIM)
    o = attend(q6, k5, v4, lam)
    o = rms_norm(o, lp["subln_w"], SUBLN_EPS) * (1.0 - lam_init)
    y_att = o.reshape(b, L, ATT_WIDTH) @ lp["w_att_out"]
    g_ssm, g_att = jnp.split(jax.nn.sigmoid(gates), 2, axis=-1)
    x = x + (g_ssm * y_ssm + g_att * y_att) @ lp["w_o"]
    gate, up = jnp.split(rms_norm(x, lp["norm_ffn_w"]) @ lp["w_ffn_in"], 2, axis=-1)
    x = x + (jax.nn.silu(gate) * up) @ lp["w_ffn_out"]
    k_rows = k.reshape(b, L, N_KV_HEADS, 2 * ATT_HEAD_DIM)
    return x, k_rows, v4, conv_state, ssm_state


def setup_inputs(seed: int = 0) -> dict:
    key = jax.random.key(seed)
    ks = jax.random.split(key, 27)
    f32 = jnp.float32
    n_pages = PAST_LEN // PAGE_SIZE
    n_pool = (DEC_BATCH * n_pages * 5) // 4

    def dense(k, fan_in, shape):
        return jax.random.normal(k, shape, f32) * fan_in ** -0.5

    def gain(k, shape):
        return 1.0 + 0.02 * jax.random.normal(k, shape, f32)

    dt0 = jnp.exp(jax.random.uniform(ks[11], (DEPTH, N_SSM_HEADS), f32, math.log(1e-3), math.log(1e-1)))
    return {
        "x_prompt": jax.random.normal(ks[0], (BATCH, SEQ, D_MODEL), f32),
        "x_sample": jax.random.normal(ks[1], (DEC_BATCH, DEC_SEQ, D_MODEL), f32),
        "cache_k": jax.random.normal(ks[2], (DEPTH, n_pool, PAGE_SIZE, N_KV_HEADS, 2 * ATT_HEAD_DIM), f32),
        "cache_v": jax.random.normal(ks[3], (DEPTH, n_pool, PAGE_SIZE, N_KV_HEADS, 2 * ATT_HEAD_DIM), f32),
        "state_conv": jax.random.normal(ks[4], (DEPTH, DEC_BATCH, D_CONV - 1, CONV_DIM), f32),
        "state_ssm": 0.1 * jax.random.normal(ks[5], (DEPTH, DEC_BATCH, N_SSM_HEADS, SSM_HEAD_DIM, D_STATE), f32),
        "page_table": jax.random.permutation(ks[6], n_pool)[:DEC_BATCH * n_pages].reshape(DEC_BATCH, n_pages).astype(jnp.int32),
        "norm_mix_w": gain(ks[7], (DEPTH, D_MODEL)),
        "w_in": dense(ks[8], D_MODEL, (DEPTH, D_MODEL, IN_WIDTH)),
        "conv_w": dense(ks[9], D_CONV, (DEPTH, D_CONV, CONV_DIM)),
        "conv_b": 0.01 * jax.random.normal(ks[10], (DEPTH, CONV_DIM), f32),
        "dt_bias": dt0 + jnp.log(-jnp.expm1(-dt0)),
        "a_log": jnp.log(jax.random.uniform(ks[12], (DEPTH, N_SSM_HEADS), f32, 1.0, 16.0)),
        "d_skip": 1.0 + 0.1 * jax.random.normal(ks[13], (DEPTH, N_SSM_HEADS), f32),
        "ssm_norm_w": gain(ks[14], (DEPTH, D_INNER)),
        "w_ssm_out": dense(ks[15], D_INNER, (DEPTH, D_INNER, D_MODEL)),
        "lambda_q1": 0.1 * jax.random.normal(ks[16], (DEPTH, ATT_HEAD_DIM), f32),
        "lambda_k1": 0.1 * jax.random.normal(ks[17], (DEPTH, ATT_HEAD_DIM), f32),
        "lambda_q2": 0.1 * jax.random.normal(ks[18], (DEPTH, ATT_HEAD_DIM), f32),
        "lambda_k2": 0.1 * jax.random.normal(ks[19], (DEPTH, ATT_HEAD_DIM), f32),
        "subln_w": gain(ks[20], (DEPTH, 2 * ATT_HEAD_DIM)),
        "w_att_out": dense(ks[21], ATT_WIDTH, (DEPTH, ATT_WIDTH, D_MODEL)),
        "w_o": dense(ks[22], D_MODEL, (DEPTH, D_MODEL, D_MODEL)),
        "norm_ffn_w": gain(ks[23], (DEPTH, D_MODEL)),
        "w_ffn_in": dense(ks[24], D_MODEL, (DEPTH, D_MODEL, 2 * D_FF)),
        "w_ffn_out": dense(ks[25], D_FF, (DEPTH, D_FF, D_MODEL)),
        "norm_final_w": gain(ks[26], (D_MODEL,)),
    }


def reference(x_prompt, x_sample, cache_k, cache_v, state_conv, state_ssm, page_table,
              norm_mix_w, w_in, conv_w, conv_b, dt_bias, a_log, d_skip, ssm_norm_w, w_ssm_out,
              lambda_q1, lambda_k1, lambda_q2, lambda_k2, subln_w, w_att_out, w_o,
              norm_ffn_w, w_ffn_in, w_ffn_out, norm_final_w):
    bp = x_prompt.shape[0]
    bs = x_sample.shape[0]
    xp, xs = x_prompt, x_sample
    kp_l, vp_l, cp_l, hp_l, ks_l, vs_l, cs_l, hs_l = [], [], [], [], [], [], [], []
    for l in range(DEPTH):
        lp = {
            "norm_mix_w": norm_mix_w[l], "w_in": w_in[l], "conv_w": conv_w[l], "conv_b": conv_b[l],
            "dt_bias": dt_bias[l], "a_log": a_log[l], "d_skip": d_skip[l], "ssm_norm_w": ssm_norm_w[l],
            "w_ssm_out": w_ssm_out[l], "lambda_q1": lambda_q1[l], "lambda_k1": lambda_k1[l],
            "lambda_q2": lambda_q2[l], "lambda_k2": lambda_k2[l], "subln_w": subln_w[l],
            "w_att_out": w_att_out[l], "w_o": w_o[l], "norm_ffn_w": norm_ffn_w[l],
            "w_ffn_in": w_ffn_in[l], "w_ffn_out": w_ffn_out[l],
        }
        li = lambda_init(l)
        conv0 = jnp.zeros((bp, D_CONV - 1, CONV_DIM), xp.dtype)
        h00 = jnp.zeros((bp, N_SSM_HEADS, SSM_HEAD_DIM, D_STATE), xp.dtype)
        xp, kp, vp, cp, hp = trunk_layer(xp, conv0, h00, attend_prompt, lp, li)
        k_past = cache_k[l][page_table].reshape(bs, -1, N_KV_HEADS, 2 * ATT_HEAD_DIM)
        v_past = cache_v[l][page_table].reshape(bs, -1, N_KV_HEADS, 2 * ATT_HEAD_DIM)
        attend_s = functools.partial(attend_paged, k_past=k_past, v_past=v_past)
        xs, kn, vn, cn, hn = trunk_layer(xs, state_conv[l], state_ssm[l], attend_s, lp, li)
        kp_l.append(kp); vp_l.append(vp); cp_l.append(cp); hp_l.append(hp)
        ks_l.append(kn); vs_l.append(vn); cs_l.append(cn); hs_l.append(hn)
    y_prompt = rms_norm(xp, norm_final_w)
    y_sample = rms_norm(xs, norm_final_w)
    return (y_prompt, y_sample,
            jnp.stack(kp_l), jnp.stack(vp_l), jnp.stack(cp_l), jnp.stack(hp_l),
            jnp.stack(ks_l), jnp.stack(vs_l), jnp.stack(cs_l), jnp.stack(hs_l))
```

```python
import functools
import math

import jax
import jax.numpy as jnp
import numpy as np
from jax import lax
from jax.experimental import pallas as pl
from jax.experimental.pallas import tpu as pltpu

F32 = jnp.float32
BF16 = jnp.bfloat16

D_MODEL = 1024
D_INNER = 2048
SSM_HEAD_DIM = 64
N_SSM_HEADS = 32
N_SSM_GROUPS = 4
D_STATE = 128
D_CONV = 4
CONV_DIM = 3072
ATT_HEAD_DIM = 64
N_KV_HEADS = 4
ATT_WIDTH = 1024
KV_WIDTH = 512
D_FF = 2816
PAGE_SIZE = 128
NORM_EPS = 1e-6
SUBLN_EPS = 1e-5
LAMBDA_INIT = 0.8 - 0.6 * math.exp(-0.3 * 0)

LANES = 128
CHUNK = 128
DT_PAD = LANES
VMEM_LIMIT = 56 * 1024 * 1024

SEG_Z, SEG_XBC, SEG_Q, SEG_K, SEG_V, SEG_G, SEG_DT = 0, 2048, 5120, 6144, 6656, 7168, 9216
W_ALL = SEG_DT + DT_PAD


def _resident(shape):
    nd = len(shape)
    return pl.BlockSpec(shape, lambda *_: (0,) * nd, pipeline_mode=pl.Buffered(1))


def _silu(x):
    return x * jax.nn.sigmoid(x)


def _inproj_kernel(x_ref, g_ref, w_ref, z_ref, xbc_ref, q_ref, k_ref, v_ref, gt_ref, dt_ref):
    x = x_ref[...]
    ms = jnp.mean(x * x, axis=-1, keepdims=True)
    h = (x * lax.rsqrt(ms + NORM_EPS) * g_ref[...]).astype(BF16)
    segs = ((z_ref, SEG_Z, D_INNER), (xbc_ref, SEG_XBC, CONV_DIM), (q_ref, SEG_Q, ATT_WIDTH),
            (k_ref, SEG_K, KV_WIDTH), (v_ref, SEG_V, KV_WIDTH), (gt_ref, SEG_G, 2 * D_MODEL),
            (dt_ref, SEG_DT, DT_PAD))
    for ref, start, width in segs:
        step = min(width, 512)
        for c in range(0, width, step):
            r = jnp.dot(h, w_ref[:, start + c:start + c + step], preferred_element_type=F32)
            ref[:, c:c + step] = r.astype(ref.dtype)


def _inproj(x2d, gain, w_all, tm=512):
    n = x2d.shape[0]
    row = lambda w: pl.BlockSpec((tm, w), lambda i: (i, 0))
    outs = [(D_INNER, BF16), (CONV_DIM, BF16), (ATT_WIDTH, BF16), (KV_WIDTH, F32), (KV_WIDTH, F32),
            (2 * D_MODEL, BF16), (DT_PAD, F32)]
    return pl.pallas_call(
        _inproj_kernel,
        grid=(n // tm,),
        in_specs=[row(D_MODEL), _resident((1, D_MODEL)), _resident((D_MODEL, W_ALL))],
        out_specs=[row(w) for w, _ in outs],
        out_shape=[jax.ShapeDtypeStruct((n, w), d) for w, d in outs],
        compiler_params=pltpu.CompilerParams(dimension_semantics=("parallel",), vmem_limit_bytes=VMEM_LIMIT),
        name="in_proj",
    )(x2d, gain, w_all)


def _split3_packed(v):
    a = v.astype(BF16).astype(F32)
    r = v - a
    b = r.astype(BF16).astype(F32)
    c = (r - b).astype(BF16).astype(F32)
    packed = a + pltpu.roll(b, N_SSM_HEADS, axis=1) + pltpu.roll(c, 2 * N_SSM_HEADS, axis=1)
    return packed.astype(BF16)


def _ssd_kernel(xbc_ref, dt_ref, pre_ref, st0_ref, cw_ref, cb_ref, dtb_ref, alog_ref, dsk_ref, e64_ref, e128_ref,
                y_ref, conv_ref, st_ref, xp_sc, act_sc, st_sc, cst_sc, *, t_in, n_chunks):
    c = pl.program_id(1)
    T = CHUNK
    row_pad = 8

    @pl.when(c == 0)
    def _():
        xp_sc[...] = jnp.zeros_like(xp_sc)
        xp_sc[row_pad - (D_CONV - 1):row_pad, :] = pre_ref[0]
        for blk in range(D_INNER // LANES):
            st_sc[:, blk * LANES:(blk + 1) * LANES] = st0_ref[0, blk * LANES:(blk + 1) * LANES, :].T

    xp_sc[row_pad:row_pad + t_in, :] = xbc_ref[0].astype(F32)

    cblk = 256
    for j in range(CONV_DIM // cblk):
        cs_ = slice(j * cblk, (j + 1) * cblk)
        acc = cb_ref[:, cs_]
        for k in range(D_CONV):
            lo = row_pad - (D_CONV - 1) + k
            acc = acc + xp_sc[lo:lo + T, cs_] * cw_ref[k:k + 1, cs_]
        act_sc[:, cs_] = _silu(acc)

    lane = lax.broadcasted_iota(jnp.int32, (T, LANES), 1)
    rowi = lax.broadcasted_iota(jnp.int32, (T, LANES), 0)
    if t_in < T:
        dt_raw = jnp.concatenate([dt_ref[0], jnp.zeros((T - t_in, DT_PAD), F32)], axis=0)
    else:
        dt_raw = dt_ref[0]
    v = dt_raw + dtb_ref[...]
    softplus = jnp.maximum(v, 0.0) + jnp.log1p(jnp.exp(-jnp.abs(v)))
    dt = jnp.where((lane < N_SSM_HEADS) & (rowi < t_in), softplus, 0.0)
    la = dt * (-jnp.exp(alog_ref[...]))
    cs = la
    d = 1
    while d < T:
        cs = cs + jnp.where(rowi >= d, pltpu.roll(cs, d, axis=0), 0.0)
        d *= 2
    cst_sc[...] = cs.T
    x_dt = _split3_packed(dt)
    x_cs = _split3_packed(cs)
    causal = rowi >= lane
    lo_half = lane < SSM_HEAD_DIM

    heads_per_group = N_SSM_HEADS // N_SSM_GROUPS
    gw = heads_per_group * SSM_HEAD_DIM
    for g in range(N_SSM_GROUPS):
        bm = act_sc[:, D_INNER + g * D_STATE:D_INNER + (g + 1) * D_STATE]
        cm = act_sc[:, D_INNER + N_SSM_GROUPS * D_STATE + g * D_STATE:
                    D_INNER + N_SSM_GROUPS * D_STATE + (g + 1) * D_STATE]
        bm_b = bm.astype(BF16)
        cm_b = cm.astype(BF16)
        bmt_b = bm.T.astype(BF16)
        cb = lax.dot_general(cm_b, bm_b, (((1,), (1,)), ((), ())), preferred_element_type=F32)
        gs = slice(g * gw, (g + 1) * gw)
        dtx = jnp.dot(x_dt, e64_ref[:, gs], preferred_element_type=F32)
        csx = jnp.dot(x_cs, e64_ref[:, gs], preferred_element_type=F32)
        cs_last = csx[T - 1:T, :]
        xs = act_sc[:, gs]
        xd = xs * dtx
        xdd_b = (xd * jnp.exp(cs_last - csx)).astype(BF16)
        xd_b = xd.astype(BF16)
        st_old = st_sc[:, gs]
        y_off = jnp.dot(cm_b, st_old.astype(BF16), preferred_element_type=F32) * jnp.exp(csx)
        st_sc[:, gs] = st_old * jnp.exp(cs_last) + jnp.dot(bmt_b, xdd_b, preferred_element_type=F32)
        y_skip = xs * dsk_ref[:, gs]
        for pr in range(heads_per_group // 2):
            ms = []
            for hh in range(2):
                h = g * heads_per_group + pr * 2 + hh
                cs_col = jnp.dot(x_cs, e128_ref[:, h * LANES:(h + 1) * LANES], preferred_element_type=F32)
                seg = jnp.where(causal, cs_col - cst_sc[h:h + 1, :], -jnp.inf)
                ms.append((cb * jnp.exp(seg)).astype(BF16))
            ps = slice(pr * LANES, (pr + 1) * LANES)
            xd_p = xd_b[:, ps]
            zero = jnp.zeros_like(xd_p)
            rhs = jnp.concatenate([jnp.where(lo_half, xd_p, zero), jnp.where(lo_half, zero, xd_p)], axis=0)
            y_diag = jnp.dot(jnp.concatenate(ms, axis=1), rhs, preferred_element_type=F32)
            y = y_diag + y_off[:, ps] + y_skip[:, ps]
            y_ref[0, :, g * gw + pr * LANES:g * gw + (pr + 1) * LANES] = y[:t_in].astype(y_ref.dtype)

    @pl.when(c == n_chunks - 1)
    def _():
        conv_ref[0] = xp_sc[row_pad + t_in - (D_CONV - 1):row_pad + t_in, :]
        for blk in range(D_INNER // LANES):
            st_ref[0, blk * LANES:(blk + 1) * LANES, :] = st_sc[:, blk * LANES:(blk + 1) * LANES].T

    if n_chunks > 1:
        xp_sc[0:row_pad, :] = xp_sc[T:T + row_pad, :]


def _expansion_matrices():
    e64 = np.zeros((LANES, D_INNER), np.float32)
    e128 = np.zeros((LANES, N_SSM_HEADS * LANES), np.float32)
    for part in range(3):
        for h in range(N_SSM_HEADS):
            e64[part * N_SSM_HEADS + h, h * SSM_HEAD_DIM:(h + 1) * SSM_HEAD_DIM] = 1.0
            e128[part * N_SSM_HEADS + h, h * LANES:(h + 1) * LANES] = 1.0
    return jnp.asarray(e64, BF16), jnp.asarray(e128, BF16)


def _ssd(xbc, dt, prefix, st0, conv_w, conv_b, dt_bias, a_log, d_skip_x):
    b, l, _ = xbc.shape
    t_in = min(l, CHUNK)
    n_chunks = l // t_in
    e64, e128 = _expansion_matrices()
    per_b = lambda shape: pl.BlockSpec((1,) + shape, lambda i, c: (i, 0, 0))
    per_chunk = lambda w: pl.BlockSpec((1, t_in, w), lambda i, c: (i, c, 0))
    return pl.pallas_call(
        functools.partial(_ssd_kernel, t_in=t_in, n_chunks=n_chunks),
        grid=(b, n_chunks),
        in_specs=[per_chunk(CONV_DIM), per_chunk(DT_PAD), per_b((D_CONV - 1, CONV_DIM)), per_b((D_INNER, D_STATE)),
                  _resident((D_CONV, CONV_DIM)), _resident((1, CONV_DIM)), _resident((1, DT_PAD)),
                  _resident((1, DT_PAD)), _resident((1, D_INNER)), _resident(e64.shape), _resident(e128.shape)],
        out_specs=[per_chunk(D_INNER), per_b((D_CONV - 1, CONV_DIM)), per_b((D_INNER, D_STATE))],
        out_shape=[jax.ShapeDtypeStruct((b, l, D_INNER), BF16),
                   jax.ShapeDtypeStruct((b, D_CONV - 1, CONV_DIM), F32),
                   jax.ShapeDtypeStruct((b, D_INNER, D_STATE), F32)],
        scratch_shapes=[pltpu.VMEM((8 + CHUNK, CONV_DIM), F32), pltpu.VMEM((CHUNK, CONV_DIM), F32),
                        pltpu.VMEM((D_STATE, D_INNER), F32), pltpu.VMEM((LANES, CHUNK), F32)],
        compiler_params=pltpu.CompilerParams(dimension_semantics=("parallel", "arbitrary"),
                                             vmem_limit_bytes=VMEM_LIMIT),
        name="ssd",
    )(xbc, dt, prefix, st0, conv_w, conv_b, dt_bias, a_log, d_skip_x, e64, e128)


def _lambda(lam_ref):
    lv = lam_ref[...]
    s1 = jnp.sum(lv[0:1] * lv[1:2], axis=-1, keepdims=True)
    s2 = jnp.sum(lv[2:3] * lv[3:4], axis=-1, keepdims=True)
    return jnp.exp(s1) - jnp.exp(s2) + LAMBDA_INIT


def _softmax_step(s, v_b, m_ref, l_ref, acc_ref, rows):
    m_prev = m_ref[rows, :]
    m_new = jnp.maximum(m_prev, jnp.max(s, axis=-1, keepdims=True))
    alpha = jnp.exp(m_prev - m_new)
    p = jnp.exp(s - m_new)
    l_ref[rows, :] = alpha * l_ref[rows, :] + jnp.sum(p, axis=-1, keepdims=True)
    acc_ref[rows, :] = alpha * acc_ref[rows, :] + jnp.dot(p.astype(BF16), v_b, preferred_element_type=F32)
    m_ref[rows, :] = m_new


def _diff_out(acc1, l1, acc2, l2, lam, sw):
    o = acc1 / l1 - lam * (acc2 / l2)
    ms = jnp.mean(o * o, axis=-1, keepdims=True)
    return o * lax.rsqrt(ms + SUBLN_EPS) * sw * (1.0 - LAMBDA_INIT)


def _map_split_rows(q0, q1, lane_lo):
    z = jnp.zeros_like(q0)
    return jnp.concatenate([jnp.where(lane_lo, q0, z), jnp.where(lane_lo, z, q0),
                            jnp.where(lane_lo, q1, z), jnp.where(lane_lo, z, q1)], axis=0)


def _attn_prompt_kernel(lam_ref, q_ref, k_ref, v_ref, sw_ref, o_ref, k_sc, v_sc, lhs_sc, m_sc, l_sc, acc_sc, *, tq, seq):
    k_sc[...] = k_ref[0].astype(BF16)
    v_sc[...] = v_ref[0].astype(BF16)
    lam = _lambda(lam_ref)
    sw = sw_ref[...]
    lane_lo = lax.broadcasted_iota(jnp.int32, (tq, LANES), 1) < ATT_HEAD_DIM
    rr = lax.broadcasted_iota(jnp.int32, (4 * tq, tq), 0) & (tq - 1)
    cc = lax.broadcasted_iota(jnp.int32, (4 * tq, tq), 1)
    causal = cc <= rr
    nt = (((1,), (1,)), ((), ()))
    all_rows = slice(0, 4 * tq)

    def q_block(qi, carry):
        q0 = q_ref[0, pl.ds(pl.multiple_of(qi * tq, tq), tq), 0:LANES]
        q1 = q_ref[0, pl.ds(pl.multiple_of(qi * tq, tq), tq), LANES:2 * LANES]
        lhs_sc[...] = _map_split_rows(q0, q1, lane_lo)
        m_sc[...] = jnp.full_like(m_sc, -jnp.inf)
        l_sc[...] = jnp.zeros_like(l_sc)
        acc_sc[...] = jnp.zeros_like(acc_sc)

        def kv_block(kj, carry2):
            ks = pl.ds(pl.multiple_of(kj * tq, tq), tq)
            s = lax.dot_general(lhs_sc[...], k_sc[ks, :], nt, preferred_element_type=F32)
            _softmax_step(s, v_sc[ks, :], m_sc, l_sc, acc_sc, all_rows)
            return carry2

        lax.fori_loop(0, qi, kv_block, 0)
        ks = pl.ds(pl.multiple_of(qi * tq, tq), tq)
        s = lax.dot_general(lhs_sc[...], k_sc[ks, :], nt, preferred_element_type=F32)
        _softmax_step(jnp.where(causal, s, -jnp.inf), v_sc[ks, :], m_sc, l_sc, acc_sc, all_rows)
        for r in range(2):
            a1 = slice((2 * r) * tq, (2 * r + 1) * tq)
            a2 = slice((2 * r + 1) * tq, (2 * r + 2) * tq)
            o = _diff_out(acc_sc[a1, :], l_sc[a1, :], acc_sc[a2, :], l_sc[a2, :], lam, sw)
            o_ref[0, pl.ds(pl.multiple_of(qi * tq, tq), tq), r * LANES:(r + 1) * LANES] = o.astype(o_ref.dtype)
        return carry

    lax.fori_loop(0, seq // tq, q_block, 0)


def _attn_prompt(q, k, v, lam_vecs, subln_w, tq=256):
    b, seq, _ = q.shape
    hw = 2 * ATT_HEAD_DIM
    return pl.pallas_call(
        functools.partial(_attn_prompt_kernel, tq=tq, seq=seq),
        grid=(b, N_KV_HEADS),
        in_specs=[_resident(lam_vecs.shape),
                  pl.BlockSpec((1, seq, 2 * hw), lambda i, h: (i, 0, h)),
                  pl.BlockSpec((1, seq, hw), lambda i, h: (i, 0, h)),
                  pl.BlockSpec((1, seq, hw), lambda i, h: (i, 0, h)),
                  _resident((1, hw))],
        out_specs=pl.BlockSpec((1, seq, 2 * hw), lambda i, h: (i, 0, h)),
        out_shape=jax.ShapeDtypeStruct((b, seq, ATT_WIDTH), BF16),
        scratch_shapes=[pltpu.VMEM((seq, hw), BF16), pltpu.VMEM((seq, hw), BF16), pltpu.VMEM((4 * tq, hw), BF16),
                        pltpu.VMEM((4 * tq, 1), F32), pltpu.VMEM((4 * tq, 1), F32), pltpu.VMEM((4 * tq, hw), F32)],
        compiler_params=pltpu.CompilerParams(dimension_semantics=("parallel", "parallel"),
                                             vmem_limit_bytes=VMEM_LIMIT),
        name="attn_prompt",
    )(lam_vecs, q, k, v, subln_w)


def _attn_paged_kernel(pt_ref, lam_ref, q_ref, kn_ref, vn_ref, sw_ref, *rest, pages_per_step, n_steps, t_new):
    k_pages = rest[:pages_per_step]
    v_pages = rest[pages_per_step:2 * pages_per_step]
    o_ref, k_sc, v_sc, lhs_sc, m_sc, l_sc, acc_sc = rest[2 * pages_per_step:]
    step = pl.program_id(1)
    hw = 2 * ATT_HEAD_DIM
    rows_h = 4 * t_new
    nt = (((1,), (1,)), ((), ()))

    @pl.when(step == 0)
    def _():
        lane_lo = lax.broadcasted_iota(jnp.int32, (t_new, LANES), 1) < ATT_HEAD_DIM
        for h in range(N_KV_HEADS):
            q0 = q_ref[0, :, (2 * h) * hw:(2 * h + 1) * hw].astype(F32)
            q1 = q_ref[0, :, (2 * h + 1) * hw:(2 * h + 2) * hw].astype(F32)
            lhs_sc[h * rows_h:(h + 1) * rows_h, :] = _map_split_rows(q0, q1, lane_lo).astype(BF16)
        m_sc[...] = jnp.full_like(m_sc, -jnp.inf)
        l_sc[...] = jnp.zeros_like(l_sc)
        acc_sc[...] = jnp.zeros_like(acc_sc)

    for i in range(pages_per_step):
        k_sc[i * PAGE_SIZE:(i + 1) * PAGE_SIZE, :] = k_pages[i][0].astype(BF16)
        v_sc[i * PAGE_SIZE:(i + 1) * PAGE_SIZE, :] = v_pages[i][0].astype(BF16)
    for h in range(N_KV_HEADS):
        rows = slice(h * rows_h, (h + 1) * rows_h)
        cols = slice(h * hw, (h + 1) * hw)
        s = lax.dot_general(lhs_sc[rows, :], k_sc[:, cols], nt, preferred_element_type=F32)
        _softmax_step(s, v_sc[:, cols], m_sc, l_sc, acc_sc, rows)

    @pl.when(step == n_steps - 1)
    def _():
        pad = jnp.zeros((PAGE_SIZE - t_new, KV_WIDTH), F32)
        kn = jnp.concatenate([kn_ref[0], pad], axis=0).astype(BF16)
        vn = jnp.concatenate([vn_ref[0], pad], axis=0).astype(BF16)
        tt = lax.broadcasted_iota(jnp.int32, (rows_h, PAGE_SIZE), 0) & (t_new - 1)
        ss = lax.broadcasted_iota(jnp.int32, (rows_h, PAGE_SIZE), 1)
        causal = ss <= tt
        lam = _lambda(lam_ref)
        sw = sw_ref[...]
        for h in range(N_KV_HEADS):
            rows = slice(h * rows_h, (h + 1) * rows_h)
            cols = slice(h * hw, (h + 1) * hw)
            s = lax.dot_general(lhs_sc[rows, :], kn[:, cols], nt, preferred_element_type=F32)
            _softmax_step(jnp.where(causal, s, -jnp.inf), vn[:, cols], m_sc, l_sc, acc_sc, rows)
            for r in range(2):
                a1 = slice(h * rows_h + (2 * r) * t_new, h * rows_h + (2 * r + 1) * t_new)
                a2 = slice(h * rows_h + (2 * r + 1) * t_new, h * rows_h + (2 * r + 2) * t_new)
                o = _diff_out(acc_sc[a1, :], l_sc[a1, :], acc_sc[a2, :], l_sc[a2, :], lam, sw)
                o_ref[0, :, (2 * h + r) * hw:(2 * h + r + 1) * hw] = o.astype(o_ref.dtype)


def _attn_paged(q, k_new, v_new, cache_k, cache_v, page_table, lam_vecs, subln_w, pages_per_step=8):
    b, t_new, _ = q.shape
    n_pages = page_table.shape[1]
    n_steps = n_pages // pages_per_step
    hw = 2 * ATT_HEAD_DIM
    pt_flat = page_table.reshape(-1)

    def page_spec(i):
        return pl.BlockSpec((1, PAGE_SIZE, KV_WIDTH),
                            lambda bi, s, pt: (pt[bi * n_pages + s * pages_per_step + i], 0, 0))

    per_b = lambda w: pl.BlockSpec((1, t_new, w), lambda bi, s, pt: (bi, 0, 0))
    const = lambda shape: pl.BlockSpec(shape, lambda bi, s, pt: (0,) * len(shape))
    rows = N_KV_HEADS * 4 * t_new
    grid_spec = pltpu.PrefetchScalarGridSpec(
        num_scalar_prefetch=1,
        grid=(b, n_steps),
        in_specs=[const(lam_vecs.shape), per_b(ATT_WIDTH), per_b(KV_WIDTH), per_b(KV_WIDTH), const((1, hw))]
                 + [page_spec(i) for i in range(pages_per_step)] * 2,
        out_specs=per_b(ATT_WIDTH),
        scratch_shapes=[pltpu.VMEM((pages_per_step * PAGE_SIZE, KV_WIDTH), BF16),
                        pltpu.VMEM((pages_per_step * PAGE_SIZE, KV_WIDTH), BF16),
                        pltpu.VMEM((rows, hw), BF16), pltpu.VMEM((rows, 1), F32), pltpu.VMEM((rows, 1), F32),
                        pltpu.VMEM((rows, hw), F32)],
    )
    return pl.pallas_call(
        functools.partial(_attn_paged_kernel, pages_per_step=pages_per_step, n_steps=n_steps, t_new=t_new),
        grid_spec=grid_spec,
        out_shape=jax.ShapeDtypeStruct((b, t_new, ATT_WIDTH), BF16),
        compiler_params=pltpu.CompilerParams(dimension_semantics=("parallel", "arbitrary"),
                                             vmem_limit_bytes=VMEM_LIMIT),
        name="attn_paged",
    )(pt_flat, lam_vecs, q, k_new, v_new, subln_w, *([cache_k] * pages_per_step), *([cache_v] * pages_per_step))


def _mix_kernel(y_ref, z_ref, o_ref, g_ref, x_ref, nw_ref, wso_ref, wao_ref, wo_ref, out_ref):
    y = y_ref[...].astype(F32) * _silu(z_ref[...].astype(F32))
    gw = D_INNER // N_SSM_GROUPS
    parts = []
    for g in range(N_SSM_GROUPS):
        yg = y[:, g * gw:(g + 1) * gw]
        ms = jnp.mean(yg * yg, axis=-1, keepdims=True)
        parts.append((yg * lax.rsqrt(ms + NORM_EPS) * nw_ref[:, g * gw:(g + 1) * gw]).astype(BF16))
    y_ssm = jnp.dot(jnp.concatenate(parts, axis=1), wso_ref[...], preferred_element_type=F32)
    y_att = jnp.dot(o_ref[...], wao_ref[...], preferred_element_type=F32)
    gates = jax.nn.sigmoid(g_ref[...].astype(F32))
    mixed = (gates[:, :D_MODEL] * y_ssm + gates[:, D_MODEL:] * y_att).astype(BF16)
    out_ref[...] = x_ref[...] + jnp.dot(mixed, wo_ref[...], preferred_element_type=F32)


def _mix(y, z, o, gates, x, ssm_norm_w, w_ssm_out, w_att_out, w_o, tm=512):
    n = x.shape[0]
    row = lambda w: pl.BlockSpec((tm, w), lambda i: (i, 0))
    return pl.pallas_call(
        _mix_kernel,
        grid=(n // tm,),
        in_specs=[row(D_INNER), row(D_INNER), row(ATT_WIDTH), row(2 * D_MODEL), row(D_MODEL),
                  _resident((1, D_INNER)), _resident((D_INNER, D_MODEL)), _resident((ATT_WIDTH, D_MODEL)),
                  _resident((D_MODEL, D_MODEL))],
        out_specs=row(D_MODEL),
        out_shape=jax.ShapeDtypeStruct((n, D_MODEL), F32),
        compiler_params=pltpu.CompilerParams(dimension_semantics=("parallel",), vmem_limit_bytes=VMEM_LIMIT),
        name="mix",
    )(y, z, o, gates, x, ssm_norm_w, w_ssm_out, w_att_out, w_o)


def _ffn_kernel(x_ref, nw_ref, win_ref, wout_ref, fw_ref, out_ref, *, chunk):
    x = x_ref[...]
    ms = jnp.mean(x * x, axis=-1, keepdims=True)
    h = (x * lax.rsqrt(ms + NORM_EPS) * nw_ref[...]).astype(BF16)
    acc = x
    for c in range(0, D_FF, chunk):
        gate = jnp.dot(h, win_ref[:, c:c + chunk], preferred_element_type=F32)
        up = jnp.dot(h, win_ref[:, D_FF + c:D_FF + c + chunk], preferred_element_type=F32)
        hid = (_silu(gate) * up).astype(BF16)
        acc = acc + jnp.dot(hid, wout_ref[c:c + chunk, :], preferred_element_type=F32)
    ms2 = jnp.mean(acc * acc, axis=-1, keepdims=True)
    out_ref[...] = acc * lax.rsqrt(ms2 + NORM_EPS) * fw_ref[...]


def _ffn(x, norm_w, w_in, w_out, final_w, tm=512, chunk=256):
    n = x.shape[0]
    row = pl.BlockSpec((tm, D_MODEL), lambda i: (i, 0))
    return pl.pallas_call(
        functools.partial(_ffn_kernel, chunk=chunk),
        grid=(n // tm,),
        in_specs=[row, _resident((1, D_MODEL)), _resident((D_MODEL, 2 * D_FF)), _resident((D_FF, D_MODEL)),
                  _resident((1, D_MODEL))],
        out_specs=row,
        out_shape=jax.ShapeDtypeStruct((n, D_MODEL), F32),
        compiler_params=pltpu.CompilerParams(dimension_semantics=("parallel",), vmem_limit_bytes=VMEM_LIMIT),
        name="ffn",
    )(x, norm_w, w_in, w_out, final_w)


def _pack_w_in(w_in):
    cuts = np.cumsum((D_INNER, CONV_DIM, N_SSM_HEADS, ATT_WIDTH, KV_WIDTH, KV_WIDTH, 2 * D_MODEL))[:-1]
    wz, wxbc, wdt, wq, wk, wv, wg = jnp.split(w_in, [int(c) for c in cuts], axis=-1)
    wdt = jnp.pad(wdt, ((0, 0), (0, DT_PAD - N_SSM_HEADS)))
    wq = wq * (ATT_HEAD_DIM ** -0.5)
    return jnp.concatenate([wz, wxbc, wq, wk, wv, wg, wdt], axis=-1).astype(BF16)


def kernel(x_prompt, x_sample, cache_k, cache_v, state_conv, state_ssm, page_table, norm_mix_w, w_in, conv_w, conv_b, dt_bias, a_log, d_skip, ssm_norm_w, w_ssm_out, lambda_q1, lambda_k1, lambda_q2, lambda_k2, subln_w, w_att_out, w_o, norm_ffn_w, w_ffn_in, w_ffn_out, norm_final_w):
    l = 0
    w_all = _pack_w_in(w_in[l])
    pad_heads = lambda t: jnp.pad(t.reshape(1, N_SSM_HEADS), ((0, 0), (0, DT_PAD - N_SSM_HEADS)))
    lam_vecs = jnp.stack([lambda_q1[l], lambda_k1[l], lambda_q2[l], lambda_k2[l]])
    params = dict(
        gain=norm_mix_w[l].reshape(1, D_MODEL), conv_w=conv_w[l], conv_b=conv_b[l].reshape(1, CONV_DIM),
        dt_bias=pad_heads(dt_bias[l]), a_log=pad_heads(a_log[l]),
        d_skip_x=jnp.repeat(d_skip[l], SSM_HEAD_DIM).reshape(1, D_INNER),
        ssm_norm_w=ssm_norm_w[l].reshape(1, D_INNER), w_ssm_out=w_ssm_out[l].astype(BF16),
        subln_w=subln_w[l].reshape(1, 2 * ATT_HEAD_DIM), w_att_out=w_att_out[l].astype(BF16),
        w_o=w_o[l].astype(BF16), norm_ffn_w=norm_ffn_w[l].reshape(1, D_MODEL),
        w_ffn_in=w_ffn_in[l].astype(BF16), w_ffn_out=w_ffn_out[l].astype(BF16),
        norm_final_w=norm_final_w.reshape(1, D_MODEL))

    def group(x, conv0, h0, attend):
        b, seq, _ = x.shape
        n = b * seq
        x2 = x.reshape(n, D_MODEL)
        z, xbc, q, k, v, gates, dt = _inproj(x2, params["gain"], w_all)
        y, conv_state, ssm_state = _ssd(
            xbc.reshape(b, seq, CONV_DIM), dt.reshape(b, seq, DT_PAD), conv0, h0.reshape(b, D_INNER, D_STATE),
            params["conv_w"], params["conv_b"], params["dt_bias"], params["a_log"], params["d_skip_x"])
        k3 = k.reshape(b, seq, KV_WIDTH)
        v3 = v.reshape(b, seq, KV_WIDTH)
        o = attend(q.reshape(b, seq, ATT_WIDTH), k3, v3)
        x1 = _mix(y.reshape(n, D_INNER), z, o.reshape(n, ATT_WIDTH), gates, x2, params["ssm_norm_w"],
                  params["w_ssm_out"], params["w_att_out"], params["w_o"])
        out = _ffn(x1, params["norm_ffn_w"], params["w_ffn_in"], params["w_ffn_out"], params["norm_final_w"])
        kv_shape = (1, b, seq, N_KV_HEADS, 2 * ATT_HEAD_DIM)
        return (out.reshape(b, seq, D_MODEL), k.reshape(kv_shape), v.reshape(kv_shape), conv_state[None],
                ssm_state.reshape(1, b, N_SSM_HEADS, SSM_HEAD_DIM, D_STATE))

    bp = x_prompt.shape[0]
    attend_p = lambda q, k, v: _attn_prompt(q, k, v, lam_vecs, params["subln_w"])
    yp, kp, vp, cp, hp = group(x_prompt, jnp.zeros((bp, D_CONV - 1, CONV_DIM), F32),
                               jnp.zeros((bp, N_SSM_HEADS, SSM_HEAD_DIM, D_STATE), F32), attend_p)
    n_pool = cache_k.shape[1]
    ck = cache_k[l].reshape(n_pool, PAGE_SIZE, KV_WIDTH)
    cv = cache_v[l].reshape(n_pool, PAGE_SIZE, KV_WIDTH)
    attend_s = lambda q, k, v: _attn_paged(q, k, v, ck, cv, page_table, lam_vecs, params["subln_w"])
    ys, ks, vs, cs, hs = group(x_sample, state_conv[l], state_ssm[l], attend_s)
    return (yp, ys, kp, vp, cp, hp, ks, vs, cs, hs)
```

```python
import functools
import math

import jax
import jax.numpy as jnp
import numpy as np
from jax import lax
from jax.experimental import pallas as pl
from jax.experimental.pallas import tpu as pltpu

F32 = jnp.float32
BF16 = jnp.bfloat16

D_MODEL = 1024
D_INNER = 2048
SSM_HEAD_DIM = 64
N_SSM_HEADS = 32
N_SSM_GROUPS = 4
D_STATE = 128
D_CONV = 4
CONV_DIM = 3072
ATT_HEAD_DIM = 64
N_KV_HEADS = 4
ATT_WIDTH = 1024
KV_WIDTH = 512
D_FF = 2816
PAGE_SIZE = 128
HEAD_W = 2 * ATT_HEAD_DIM
NORM_EPS = 1e-6
SUBLN_EPS = 1e-5
LAMBDA_INIT = 0.8 - 0.6 * math.exp(-0.3 * 0)

LANES = 128
CHUNK = 128
DT_PAD = LANES
VMEM_LIMIT = 56 * 1024 * 1024

SEG_Z, SEG_XBC, SEG_Q, SEG_K, SEG_V, SEG_G, SEG_DT = 0, 2048, 5120, 6144, 6656, 7168, 9216
W_ALL = SEG_DT + DT_PAD


def _resident(shape):
    nd = len(shape)
    return pl.BlockSpec(shape, lambda *_: (0,) * nd, pipeline_mode=pl.Buffered(1))


def _silu(x):
    return x * jax.nn.sigmoid(x)


def _inproj_kernel(x_ref, g_ref, w_ref, z_ref, xbc_ref, q_ref, k_ref, v_ref, gt_ref, dt_ref):
    x = x_ref[...]
    ms = jnp.mean(x * x, axis=-1, keepdims=True)
    h = (x * lax.rsqrt(ms + NORM_EPS) * g_ref[...]).astype(BF16)
    segs = ((z_ref, SEG_Z, D_INNER), (xbc_ref, SEG_XBC, CONV_DIM), (q_ref, SEG_Q, ATT_WIDTH),
            (gt_ref, SEG_G, 2 * D_MODEL), (dt_ref, SEG_DT, DT_PAD))
    for ref, start, width in segs:
        step = min(width, 512)
        for c in range(0, width, step):
            r = jnp.dot(h, w_ref[:, start + c:start + c + step], preferred_element_type=F32)
            ref[:, c:c + step] = r.astype(ref.dtype)
    tm = x_ref.shape[0]
    for ref, start in ((k_ref, SEG_K), (v_ref, SEG_V)):
        r = jnp.dot(h, w_ref[:, start:start + KV_WIDTH], preferred_element_type=F32)
        for hd in range(N_KV_HEADS):
            ref[pl.ds(hd, tm, stride=N_KV_HEADS), :] = r[:, hd * HEAD_W:(hd + 1) * HEAD_W]


def _inproj(x2d, gain, w_all, tm=512):
    n = x2d.shape[0]
    row = lambda w: pl.BlockSpec((tm, w), lambda i: (i, 0))
    kv_rows = pl.BlockSpec((tm * N_KV_HEADS, HEAD_W), lambda i: (i, 0))
    kv_shape = jax.ShapeDtypeStruct((n * N_KV_HEADS, HEAD_W), F32)
    outs = [(D_INNER, BF16), (CONV_DIM, BF16), (ATT_WIDTH, BF16), None, None, (2 * D_MODEL, BF16), (DT_PAD, F32)]
    return pl.pallas_call(
        _inproj_kernel,
        grid=(n // tm,),
        in_specs=[row(D_MODEL), _resident((1, D_MODEL)), _resident((D_MODEL, W_ALL))],
        out_specs=[kv_rows if o is None else row(o[0]) for o in outs],
        out_shape=[kv_shape if o is None else jax.ShapeDtypeStruct((n, o[0]), o[1]) for o in outs],
        compiler_params=pltpu.CompilerParams(dimension_semantics=("parallel",), vmem_limit_bytes=VMEM_LIMIT),
        name="in_proj",
    )(x2d, gain, w_all)


def _split3_packed(v):
    a = v.astype(BF16).astype(F32)
    r = v - a
    b = r.astype(BF16).astype(F32)
    c = (r - b).astype(BF16).astype(F32)
    packed = a + pltpu.roll(b, N_SSM_HEADS, axis=1) + pltpu.roll(c, 2 * N_SSM_HEADS, axis=1)
    return packed.astype(BF16)


def _ssd_kernel(xbc_ref, dt_ref, pre_ref, st0_ref, cw_ref, cb_ref, dtb_ref, alog_ref, dsk_ref, e64_ref, e128_ref,
                y_ref, conv_ref, st_ref, xp_sc, act_sc, st_sc, cst_sc, *, t_in, n_chunks):
    c = pl.program_id(1)
    T = CHUNK
    row_pad = 8

    @pl.when(c == 0)
    def _():
        xp_sc[...] = jnp.zeros_like(xp_sc)
        xp_sc[row_pad - (D_CONV - 1):row_pad, :] = pre_ref[0]
        for blk in range(D_INNER // LANES):
            st_sc[:, blk * LANES:(blk + 1) * LANES] = st0_ref[0, blk * LANES:(blk + 1) * LANES, :].T

    xp_sc[row_pad:row_pad + t_in, :] = xbc_ref[0].astype(F32)

    cblk = 256
    for j in range(CONV_DIM // cblk):
        cs_ = slice(j * cblk, (j + 1) * cblk)
        acc = cb_ref[:, cs_]
        for k in range(D_CONV):
            lo = row_pad - (D_CONV - 1) + k
            acc = acc + xp_sc[lo:lo + T, cs_] * cw_ref[k:k + 1, cs_]
        act_sc[:, cs_] = _silu(acc)

    lane = lax.broadcasted_iota(jnp.int32, (T, LANES), 1)
    rowi = lax.broadcasted_iota(jnp.int32, (T, LANES), 0)
    if t_in < T:
        dt_raw = jnp.concatenate([dt_ref[0], jnp.zeros((T - t_in, DT_PAD), F32)], axis=0)
    else:
        dt_raw = dt_ref[0]
    v = dt_raw + dtb_ref[...]
    softplus = jnp.maximum(v, 0.0) + jnp.log1p(jnp.exp(-jnp.abs(v)))
    dt = jnp.where((lane < N_SSM_HEADS) & (rowi < t_in), softplus, 0.0)
    la = dt * (-jnp.exp(alog_ref[...]))
    cs = la
    d = 1
    while d < T:
        cs = cs + jnp.where(rowi >= d, pltpu.roll(cs, d, axis=0), 0.0)
        d *= 2
    cst_sc[...] = cs.T
    x_dt = _split3_packed(dt)
    x_cs = _split3_packed(cs)
    causal = rowi >= lane
    lo_half = lane < SSM_HEAD_DIM

    heads_per_group = N_SSM_HEADS // N_SSM_GROUPS
    gw = heads_per_group * SSM_HEAD_DIM
    for g in range(N_SSM_GROUPS):
        bm = act_sc[:, D_INNER + g * D_STATE:D_INNER + (g + 1) * D_STATE]
        cm = act_sc[:, D_INNER + N_SSM_GROUPS * D_STATE + g * D_STATE:
                    D_INNER + N_SSM_GROUPS * D_STATE + (g + 1) * D_STATE]
        bm_b = bm.astype(BF16)
        cm_b = cm.astype(BF16)
        bmt_b = bm.T.astype(BF16)
        cb = lax.dot_general(cm_b, bm_b, (((1,), (1,)), ((), ())), preferred_element_type=F32)
        gs = slice(g * gw, (g + 1) * gw)
        dtx = jnp.dot(x_dt, e64_ref[:, gs], preferred_element_type=F32)
        csx = jnp.dot(x_cs, e64_ref[:, gs], preferred_element_type=F32)
        cs_last = csx[T - 1:T, :]
        xs = act_sc[:, gs]
        xd = xs * dtx
        xdd_b = (xd * jnp.exp(cs_last - csx)).astype(BF16)
        xd_b = xd.astype(BF16)
        st_old = st_sc[:, gs]
        y_off = jnp.dot(cm_b, st_old.astype(BF16), preferred_element_type=F32) * jnp.exp(csx)
        st_sc[:, gs] = st_old * jnp.exp(cs_last) + jnp.dot(bmt_b, xdd_b, preferred_element_type=F32)
        y_skip = xs * dsk_ref[:, gs]
        for pr in range(heads_per_group // 2):
            ms = []
            for hh in range(2):
                h = g * heads_per_group + pr * 2 + hh
                cs_col = jnp.dot(x_cs, e128_ref[:, h * LANES:(h + 1) * LANES], preferred_element_type=F32)
                seg = jnp.where(causal, cs_col - cst_sc[h:h + 1, :], -jnp.inf)
                ms.append((cb * jnp.exp(seg)).astype(BF16))
            ps = slice(pr * LANES, (pr + 1) * LANES)
            xd_p = xd_b[:, ps]
            zero = jnp.zeros_like(xd_p)
            rhs = jnp.concatenate([jnp.where(lo_half, xd_p, zero), jnp.where(lo_half, zero, xd_p)], axis=0)
            y_diag = jnp.dot(jnp.concatenate(ms, axis=1), rhs, preferred_element_type=F32)
            y = y_diag + y_off[:, ps] + y_skip[:, ps]
            y_ref[0, :, g * gw + pr * LANES:g * gw + (pr + 1) * LANES] = y[:t_in].astype(y_ref.dtype)

    @pl.when(c == n_chunks - 1)
    def _():
        conv_ref[0] = xp_sc[row_pad + t_in - (D_CONV - 1):row_pad + t_in, :]
        for blk in range(D_INNER // LANES):
            st_ref[0, blk * LANES:(blk + 1) * LANES, :] = st_sc[:, blk * LANES:(blk + 1) * LANES].T

    if n_chunks > 1:
        xp_sc[0:row_pad, :] = xp_sc[T:T + row_pad, :]


def _expansion_matrices():
    e64 = np.zeros((LANES, D_INNER), np.float32)
    e128 = np.zeros((LANES, N_SSM_HEADS * LANES), np.float32)
    for part in range(3):
        for h in range(N_SSM_HEADS):
            e64[part * N_SSM_HEADS + h, h * SSM_HEAD_DIM:(h + 1) * SSM_HEAD_DIM] = 1.0
            e128[part * N_SSM_HEADS + h, h * LANES:(h + 1) * LANES] = 1.0
    return jnp.asarray(e64, BF16), jnp.asarray(e128, BF16)


def _ssd(xbc, dt, prefix, st0, conv_w, conv_b, dt_bias, a_log, d_skip_x):
    b, l, _ = xbc.shape
    t_in = min(l, CHUNK)
    n_chunks = l // t_in
    e64, e128 = _expansion_matrices()
    per_b = lambda shape: pl.BlockSpec((1,) + shape, lambda i, c: (i, 0, 0))
    per_chunk = lambda w: pl.BlockSpec((1, t_in, w), lambda i, c: (i, c, 0))
    return pl.pallas_call(
        functools.partial(_ssd_kernel, t_in=t_in, n_chunks=n_chunks),
        grid=(b, n_chunks),
        in_specs=[per_chunk(CONV_DIM), per_chunk(DT_PAD), per_b((D_CONV - 1, CONV_DIM)), per_b((D_INNER, D_STATE)),
                  _resident((D_CONV, CONV_DIM)), _resident((1, CONV_DIM)), _resident((1, DT_PAD)),
                  _resident((1, DT_PAD)), _resident((1, D_INNER)), _resident(e64.shape), _resident(e128.shape)],
        out_specs=[per_chunk(D_INNER), per_b((D_CONV - 1, CONV_DIM)), per_b((D_INNER, D_STATE))],
        out_shape=[jax.ShapeDtypeStruct((b, l, D_INNER), BF16),
                   jax.ShapeDtypeStruct((b, D_CONV - 1, CONV_DIM), F32),
                   jax.ShapeDtypeStruct((b, D_INNER, D_STATE), F32)],
        scratch_shapes=[pltpu.VMEM((8 + CHUNK, CONV_DIM), F32), pltpu.VMEM((CHUNK, CONV_DIM), F32),
                        pltpu.VMEM((D_STATE, D_INNER), F32), pltpu.VMEM((LANES, CHUNK), F32)],
        compiler_params=pltpu.CompilerParams(dimension_semantics=("parallel", "arbitrary"),
                                             vmem_limit_bytes=VMEM_LIMIT),
        name="ssd",
    )(xbc, dt, prefix, st0, conv_w, conv_b, dt_bias, a_log, d_skip_x, e64, e128)


def _lambda(lam_ref):
    lv = lam_ref[...]
    s1 = jnp.sum(lv[0:1] * lv[1:2], axis=-1, keepdims=True)
    s2 = jnp.sum(lv[2:3] * lv[3:4], axis=-1, keepdims=True)
    return jnp.exp(s1) - jnp.exp(s2) + LAMBDA_INIT


def _attn_prompt_kernel(lam_ref, q_ref, k_ref, v_ref, sw_ref, o_ref, kall_sc, vtall_sc, qt_sc, m_sc, l_sc, acc_sc, *,
                        tq, seq):
    h = pl.program_id(1)

    @pl.when(h == 0)
    def _():
        for hd in range(N_KV_HEADS):
            kall_sc[hd] = k_ref[pl.ds(hd, seq, stride=N_KV_HEADS), :].astype(BF16)
            for blk in range(seq // LANES):
                vb = v_ref[pl.ds(hd + blk * LANES * N_KV_HEADS, LANES, stride=N_KV_HEADS), :]
                vtall_sc[hd, :, blk * LANES:(blk + 1) * LANES] = vb.T.astype(BF16)

    k_sc = kall_sc.at[h]
    vt_sc = vtall_sc.at[h]
    lam = _lambda(lam_ref)
    sw = sw_ref[...]
    feat_lo = lax.broadcasted_iota(jnp.int32, (LANES, tq), 0) < ATT_HEAD_DIM
    key = lax.broadcasted_iota(jnp.int32, (tq, 4 * tq), 0)
    qt = lax.broadcasted_iota(jnp.int32, (tq, 4 * tq), 1) & (tq - 1)
    causal = key <= qt

    def step(s, ks):
        m_prev = m_sc[...]
        m_new = jnp.maximum(m_prev, jnp.max(s, axis=0, keepdims=True))
        alpha = jnp.exp(m_prev - m_new)
        p = jnp.exp(s - m_new)
        l_sc[...] = alpha * l_sc[...] + jnp.sum(p, axis=0, keepdims=True)
        acc_sc[...] = alpha * acc_sc[...] + jnp.dot(vt_sc[:, ks], p.astype(BF16), preferred_element_type=F32)
        m_sc[...] = m_new

    def q_block(qi, carry):
        qs = pl.ds(pl.multiple_of(qi * tq, tq), tq)
        for r in range(2):
            pair_t = q_ref[0, qs, r * LANES:(r + 1) * LANES].astype(F32).T
            zero = jnp.zeros_like(pair_t)
            qt_sc[:, (2 * r) * tq:(2 * r + 1) * tq] = jnp.where(feat_lo, pair_t, zero).astype(BF16)
            qt_sc[:, (2 * r + 1) * tq:(2 * r + 2) * tq] = jnp.where(feat_lo, zero, pair_t).astype(BF16)
        m_sc[...] = jnp.full_like(m_sc, -jnp.inf)
        l_sc[...] = jnp.zeros_like(l_sc)
        acc_sc[...] = jnp.zeros_like(acc_sc)

        def kv_block(kj, carry2):
            ks = pl.ds(pl.multiple_of(kj * tq, tq), tq)
            step(jnp.dot(k_sc[ks, :], qt_sc[...], preferred_element_type=F32), ks)
            return carry2

        lax.fori_loop(0, qi, kv_block, 0)
        s = jnp.dot(k_sc[qs, :], qt_sc[...], preferred_element_type=F32)
        step(jnp.where(causal, s, -jnp.inf), qs)
        on = acc_sc[...] / l_sc[...]
        for r in range(2):
            o_t = on[:, (2 * r) * tq:(2 * r + 1) * tq] - lam * on[:, (2 * r + 1) * tq:(2 * r + 2) * tq]
            o = o_t.T
            ms = jnp.mean(o * o, axis=-1, keepdims=True)
            o = o * lax.rsqrt(ms + SUBLN_EPS) * sw * (1.0 - LAMBDA_INIT)
            o_ref[0, qs, r * LANES:(r + 1) * LANES] = o.astype(o_ref.dtype)
        return carry

    lax.fori_loop(0, seq // tq, q_block, 0)


def _attn_prompt(q, k, v, lam_vecs, subln_w, tq=256):
    b, seq, _ = q.shape
    kv_spec = pl.BlockSpec((None, seq * N_KV_HEADS, HEAD_W), lambda i, h: (i, 0, 0))
    return pl.pallas_call(
        functools.partial(_attn_prompt_kernel, tq=tq, seq=seq),
        grid=(b, N_KV_HEADS),
        in_specs=[_resident(lam_vecs.shape),
                  pl.BlockSpec((1, seq, 2 * HEAD_W), lambda i, h: (i, 0, h)),
                  kv_spec, kv_spec, _resident((1, HEAD_W))],
        out_specs=pl.BlockSpec((1, seq, 2 * HEAD_W), lambda i, h: (i, 0, h)),
        out_shape=jax.ShapeDtypeStruct((b, seq, ATT_WIDTH), BF16),
        scratch_shapes=[pltpu.VMEM((N_KV_HEADS, seq, HEAD_W), BF16), pltpu.VMEM((N_KV_HEADS, HEAD_W, seq), BF16),
                        pltpu.VMEM((HEAD_W, 4 * tq), BF16), pltpu.VMEM((1, 4 * tq), F32),
                        pltpu.VMEM((1, 4 * tq), F32), pltpu.VMEM((HEAD_W, 4 * tq), F32)],
        compiler_params=pltpu.CompilerParams(dimension_semantics=("parallel", "arbitrary"),
                                             vmem_limit_bytes=VMEM_LIMIT),
        name="attn_prompt",
    )(lam_vecs, q, k, v, subln_w)


def _col_broadcast(row):
    return jnp.broadcast_to(row, (LANES, LANES)).T


def _attn_paged_kernel(pt_ref, lam_ref, q_ref, kn_ref, vn_ref, sw_ref, *rest, pages_per_step, n_steps, t_new):
    k_pages = rest[:pages_per_step]
    v_pages = rest[pages_per_step:2 * pages_per_step]
    o_ref, qall_sc, s_sc, m_sc, l_sc, acc_sc = rest[2 * pages_per_step:]
    step = pl.program_id(1)
    rows_pg = PAGE_SIZE * N_KV_HEADS
    head_shift = int(math.log2(N_KV_HEADS))
    col_shift = int(math.log2(4 * t_new))
    row_i = lax.broadcasted_iota(jnp.int32, (rows_pg, LANES), 0)
    col_i = lax.broadcasted_iota(jnp.int32, (rows_pg, LANES), 1)
    head_match = (row_i & (N_KV_HEADS - 1)) == (col_i >> col_shift)

    @pl.when(step == 0)
    def _():
        lane_lo = lax.broadcasted_iota(jnp.int32, (t_new, LANES), 1) < ATT_HEAD_DIM
        pieces = []
        for hr in range(2 * N_KV_HEADS):
            pair = q_ref[0, :, hr * HEAD_W:(hr + 1) * HEAD_W].astype(F32)
            zero = jnp.zeros_like(pair)
            pieces += [jnp.where(lane_lo, pair, zero), jnp.where(lane_lo, zero, pair)]
        qall_sc[...] = jnp.concatenate(pieces, axis=0).T.astype(BF16)
        m_sc[...] = jnp.full_like(m_sc, -jnp.inf)
        l_sc[...] = jnp.zeros_like(l_sc)
        acc_sc[...] = jnp.zeros_like(acc_sc)

    def update(n_blocks, rows, score_fn, value_fn):
        m_prev = m_sc[...]
        m_new = m_prev
        for i in range(n_blocks):
            s = score_fn(i)
            s_sc[i * rows_pg:i * rows_pg + rows, :] = s
            m_new = jnp.maximum(m_new, jnp.max(s, axis=0, keepdims=True))
        alpha = jnp.exp(m_prev - m_new)
        lsum = jnp.zeros_like(m_new)
        pv = jnp.zeros((LANES, HEAD_W), F32)
        for i in range(n_blocks):
            p = jnp.exp(s_sc[i * rows_pg:i * rows_pg + rows, :] - m_new)
            lsum = lsum + jnp.sum(p, axis=0, keepdims=True)
            pv = pv + jnp.dot(p.T.astype(BF16), value_fn(i), preferred_element_type=F32)
        l_sc[...] = alpha * l_sc[...] + lsum
        acc_sc[...] = _col_broadcast(alpha) * acc_sc[...] + pv
        m_sc[...] = m_new

    def page_scores(i):
        s = jnp.dot(k_pages[i][0].astype(BF16), qall_sc[...], preferred_element_type=F32)
        return jnp.where(head_match, s, -jnp.inf)

    update(pages_per_step, rows_pg, page_scores, lambda i: v_pages[i][0].astype(BF16))

    @pl.when(step == n_steps - 1)
    def _():
        pad = jnp.zeros((LANES - t_new * N_KV_HEADS, HEAD_W), F32)
        kn = jnp.concatenate([kn_ref[0], pad], axis=0).astype(BF16)
        vn = jnp.concatenate([vn_ref[0], pad], axis=0).astype(BF16)
        row_n = lax.broadcasted_iota(jnp.int32, (LANES, LANES), 0)
        col_n = lax.broadcasted_iota(jnp.int32, (LANES, LANES), 1)
        live = (((row_n & (N_KV_HEADS - 1)) == (col_n >> col_shift))
                & ((row_n >> head_shift) <= (col_n & (t_new - 1))))
        s_new = jnp.where(live, jnp.dot(kn, qall_sc[...], preferred_element_type=F32), -jnp.inf)
        update(1, LANES, lambda i: s_new, lambda i: vn)
        lam = _lambda(lam_ref)
        sw = sw_ref[...]
        on = acc_sc[...] / _col_broadcast(l_sc[...])
        for hr in range(2 * N_KV_HEADS):
            base = hr * 2 * t_new
            o = on[base:base + t_new, :] - lam * on[base + t_new:base + 2 * t_new, :]
            ms = jnp.mean(o * o, axis=-1, keepdims=True)
            o = o * lax.rsqrt(ms + SUBLN_EPS) * sw * (1.0 - LAMBDA_INIT)
            o_ref[0, :, hr * HEAD_W:(hr + 1) * HEAD_W] = o.astype(o_ref.dtype)


def _attn_paged(q, k_new, v_new, cache_k, cache_v, page_table, lam_vecs, subln_w, pages_per_step=8):
    b, t_new, _ = q.shape
    n_pages = page_table.shape[1]
    n_steps = n_pages // pages_per_step
    rows_pg = PAGE_SIZE * N_KV_HEADS
    assert N_KV_HEADS * 4 * t_new == LANES
    pt_flat = page_table.reshape(-1)

    def page_spec(i):
        return pl.BlockSpec((1, rows_pg, HEAD_W),
                            lambda bi, s, pt: (pt[bi * n_pages + s * pages_per_step + i], 0, 0))

    per_b = lambda r, w: pl.BlockSpec((1, r, w), lambda bi, s, pt: (bi, 0, 0))
    const = lambda shape: pl.BlockSpec(shape, lambda bi, s, pt: (0,) * len(shape))
    grid_spec = pltpu.PrefetchScalarGridSpec(
        num_scalar_prefetch=1,
        grid=(b, n_steps),
        in_specs=[const(lam_vecs.shape), per_b(t_new, ATT_WIDTH), per_b(t_new * N_KV_HEADS, HEAD_W),
                  per_b(t_new * N_KV_HEADS, HEAD_W), const((1, HEAD_W))]
                 + [page_spec(i) for i in range(pages_per_step)] * 2,
        out_specs=per_b(t_new, ATT_WIDTH),
        scratch_shapes=[pltpu.VMEM((HEAD_W, LANES), BF16), pltpu.VMEM((pages_per_step * rows_pg, LANES), F32),
                        pltpu.VMEM((1, LANES), F32), pltpu.VMEM((1, LANES), F32), pltpu.VMEM((LANES, HEAD_W), F32)],
    )
    return pl.pallas_call(
        functools.partial(_attn_paged_kernel, pages_per_step=pages_per_step, n_steps=n_steps, t_new=t_new),
        grid_spec=grid_spec,
        out_shape=jax.ShapeDtypeStruct((b, t_new, ATT_WIDTH), BF16),
        compiler_params=pltpu.CompilerParams(dimension_semantics=("parallel", "arbitrary"),
                                             vmem_limit_bytes=VMEM_LIMIT),
        name="attn_paged",
    )(pt_flat, lam_vecs, q, k_new, v_new, subln_w, *([cache_k] * pages_per_step), *([cache_v] * pages_per_step))


def _mix_kernel(y_ref, z_ref, o_ref, g_ref, x_ref, nw_ref, wso_ref, wao_ref, wo_ref, out_ref):
    y = y_ref[...].astype(F32) * _silu(z_ref[...].astype(F32))
    gw = D_INNER // N_SSM_GROUPS
    parts = []
    for g in range(N_SSM_GROUPS):
        yg = y[:, g * gw:(g + 1) * gw]
        ms = jnp.mean(yg * yg, axis=-1, keepdims=True)
        parts.append((yg * lax.rsqrt(ms + NORM_EPS) * nw_ref[:, g * gw:(g + 1) * gw]).astype(BF16))
    y_ssm = jnp.dot(jnp.concatenate(parts, axis=1), wso_ref[...], preferred_element_type=F32)
    y_att = jnp.dot(o_ref[...], wao_ref[...], preferred_element_type=F32)
    gates = jax.nn.sigmoid(g_ref[...].astype(F32))
    mixed = (gates[:, :D_MODEL] * y_ssm + gates[:, D_MODEL:] * y_att).astype(BF16)
    out_ref[...] = x_ref[...] + jnp.dot(mixed, wo_ref[...], preferred_element_type=F32)


def _mix(y, z, o, gates, x, ssm_norm_w, w_ssm_out, w_att_out, w_o, tm=512):
    n = x.shape[0]
    row = lambda w: pl.BlockSpec((tm, w), lambda i: (i, 0))
    return pl.pallas_call(
        _mix_kernel,
        grid=(n // tm,),
        in_specs=[row(D_INNER), row(D_INNER), row(ATT_WIDTH), row(2 * D_MODEL), row(D_MODEL),
                  _resident((1, D_INNER)), _resident((D_INNER, D_MODEL)), _resident((ATT_WIDTH, D_MODEL)),
                  _resident((D_MODEL, D_MODEL))],
        out_specs=row(D_MODEL),
        out_shape=jax.ShapeDtypeStruct((n, D_MODEL), F32),
        compiler_params=pltpu.CompilerParams(dimension_semantics=("parallel",), vmem_limit_bytes=VMEM_LIMIT),
        name="mix",
    )(y, z, o, gates, x, ssm_norm_w, w_ssm_out, w_att_out, w_o)


def _ffn_kernel(x_ref, nw_ref, win_ref, wout_ref, fw_ref, out_ref, *, chunk):
    x = x_ref[...]
    ms = jnp.mean(x * x, axis=-1, keepdims=True)
    h = (x * lax.rsqrt(ms + NORM_EPS) * nw_ref[...]).astype(BF16)
    acc = x
    for c in range(0, D_FF, chunk):
        gate = jnp.dot(h, win_ref[:, c:c + chunk], preferred_element_type=F32)
        up = jnp.dot(h, win_ref[:, D_FF + c:D_FF + c + chunk], preferred_element_type=F32)
        hid = (_silu(gate) * up).astype(BF16)
        acc = acc + jnp.dot(hid, wout_ref[c:c + chunk, :], preferred_element_type=F32)
    ms2 = jnp.mean(acc * acc, axis=-1, keepdims=True)
    out_ref[...] = acc * lax.rsqrt(ms2 + NORM_EPS) * fw_ref[...]


def _ffn(x, norm_w, w_in, w_out, final_w, tm=512, chunk=256):
    n = x.shape[0]
    row = pl.BlockSpec((tm, D_MODEL), lambda i: (i, 0))
    return pl.pallas_call(
        functools.partial(_ffn_kernel, chunk=chunk),
        grid=(n // tm,),
        in_specs=[row, _resident((1, D_MODEL)), _resident((D_MODEL, 2 * D_FF)), _resident((D_FF, D_MODEL)),
                  _resident((1, D_MODEL))],
        out_specs=row,
        out_shape=jax.ShapeDtypeStruct((n, D_MODEL), F32),
        compiler_params=pltpu.CompilerParams(dimension_semantics=("parallel",), vmem_limit_bytes=VMEM_LIMIT),
        name="ffn",
    )(x, norm_w, w_in, w_out, final_w)


def _pack_w_in(w_in):
    cuts = np.cumsum((D_INNER, CONV_DIM, N_SSM_HEADS, ATT_WIDTH, KV_WIDTH, KV_WIDTH, 2 * D_MODEL))[:-1]
    wz, wxbc, wdt, wq, wk, wv, wg = jnp.split(w_in, [int(c) for c in cuts], axis=-1)
    wdt = jnp.pad(wdt, ((0, 0), (0, DT_PAD - N_SSM_HEADS)))
    wq = wq * (ATT_HEAD_DIM ** -0.5)
    return jnp.concatenate([wz, wxbc, wq, wk, wv, wg, wdt], axis=-1).astype(BF16)


def kernel(x_prompt, x_sample, cache_k, cache_v, state_conv, state_ssm, page_table, norm_mix_w, w_in, conv_w, conv_b, dt_bias, a_log, d_skip, ssm_norm_w, w_ssm_out, lambda_q1, lambda_k1, lambda_q2, lambda_k2, subln_w, w_att_out, w_o, norm_ffn_w, w_ffn_in, w_ffn_out, norm_final_w):
    l = 0
    w_all = _pack_w_in(w_in[l])
    pad_heads = lambda t: jnp.pad(t.reshape(1, N_SSM_HEADS), ((0, 0), (0, DT_PAD - N_SSM_HEADS)))
    lam_vecs = jnp.stack([lambda_q1[l], lambda_k1[l], lambda_q2[l], lambda_k2[l]])
    params = dict(
        gain=norm_mix_w[l].reshape(1, D_MODEL), conv_w=conv_w[l], conv_b=conv_b[l].reshape(1, CONV_DIM),
        dt_bias=pad_heads(dt_bias[l]), a_log=pad_heads(a_log[l]),
        d_skip_x=jnp.repeat(d_skip[l], SSM_HEAD_DIM).reshape(1, D_INNER),
        ssm_norm_w=ssm_norm_w[l].reshape(1, D_INNER), w_ssm_out=w_ssm_out[l].astype(BF16),
        subln_w=subln_w[l].reshape(1, 2 * ATT_HEAD_DIM), w_att_out=w_att_out[l].astype(BF16),
        w_o=w_o[l].astype(BF16), norm_ffn_w=norm_ffn_w[l].reshape(1, D_MODEL),
        w_ffn_in=w_ffn_in[l].astype(BF16), w_ffn_out=w_ffn_out[l].astype(BF16),
        norm_final_w=norm_final_w.reshape(1, D_MODEL))

    def group(x, conv0, h0, attend):
        b, seq, _ = x.shape
        n = b * seq
        x2 = x.reshape(n, D_MODEL)
        z, xbc, q, k, v, gates, dt = _inproj(x2, params["gain"], w_all)
        y, conv_state, ssm_state = _ssd(
            xbc.reshape(b, seq, CONV_DIM), dt.reshape(b, seq, DT_PAD), conv0, h0.reshape(b, D_INNER, D_STATE),
            params["conv_w"], params["conv_b"], params["dt_bias"], params["a_log"], params["d_skip_x"])
        k3 = k.reshape(b, seq * N_KV_HEADS, HEAD_W)
        v3 = v.reshape(b, seq * N_KV_HEADS, HEAD_W)
        o = attend(q.reshape(b, seq, ATT_WIDTH), k3, v3)
        x1 = _mix(y.reshape(n, D_INNER), z, o.reshape(n, ATT_WIDTH), gates, x2, params["ssm_norm_w"],
                  params["w_ssm_out"], params["w_att_out"], params["w_o"])
        out = _ffn(x1, params["norm_ffn_w"], params["w_ffn_in"], params["w_ffn_out"], params["norm_final_w"])
        kv_shape = (1, b, seq, N_KV_HEADS, 2 * ATT_HEAD_DIM)
        return (out.reshape(b, seq, D_MODEL), k.reshape(kv_shape), v.reshape(kv_shape), conv_state[None],
                ssm_state.reshape(1, b, N_SSM_HEADS, SSM_HEAD_DIM, D_STATE))

    bp = x_prompt.shape[0]
    attend_p = lambda q, k, v: _attn_prompt(q, k, v, lam_vecs, params["subln_w"])
    yp, kp, vp, cp, hp = group(x_prompt, jnp.zeros((bp, D_CONV - 1, CONV_DIM), F32),
                               jnp.zeros((bp, N_SSM_HEADS, SSM_HEAD_DIM, D_STATE), F32), attend_p)
    n_pool = cache_k.shape[1]
    ck = cache_k[l].reshape(n_pool, PAGE_SIZE * N_KV_HEADS, HEAD_W)
    cv = cache_v[l].reshape(n_pool, PAGE_SIZE * N_KV_HEADS, HEAD_W)
    attend_s = lambda q, k, v: _attn_paged(q, k, v, ck, cv, page_table, lam_vecs, params["subln_w"])
    ys, ks, vs, cs, hs = group(x_sample, state_conv[l], state_ssm[l], attend_s)
    return (yp, ys, kp, vp, cp, hp, ks, vs, cs, hs)
```

```python
import functools
import math

import jax
import jax.numpy as jnp
import numpy as np
from jax import lax
from jax.experimental import pallas as pl
from jax.experimental.pallas import tpu as pltpu

F32 = jnp.float32
BF16 = jnp.bfloat16

D_MODEL = 1024
D_INNER = 2048
SSM_HEAD_DIM = 64
N_SSM_HEADS = 32
N_SSM_GROUPS = 4
D_STATE = 128
D_CONV = 4
CONV_DIM = 3072
ATT_HEAD_DIM = 64
N_KV_HEADS = 4
ATT_WIDTH = 1024
KV_WIDTH = 512
D_FF = 2816
PAGE_SIZE = 128
HEAD_W = 2 * ATT_HEAD_DIM
NORM_EPS = 1e-6
SUBLN_EPS = 1e-5
LAMBDA_INIT = 0.8 - 0.6 * math.exp(-0.3 * 0)

LANES = 128
CHUNK = 128
DT_PAD = LANES
VMEM_LIMIT = 56 * 1024 * 1024

SEG_Z, SEG_XBC, SEG_Q, SEG_K, SEG_V, SEG_G, SEG_DT = 0, 2048, 5120, 6144, 6656, 7168, 9216
W_ALL = SEG_DT + DT_PAD


def _resident(shape):
    nd = len(shape)
    return pl.BlockSpec(shape, lambda *_: (0,) * nd, pipeline_mode=pl.Buffered(1))


def _silu(x):
    return x * jax.nn.sigmoid(x)


def _inproj_kernel(x_ref, g_ref, w_ref, z_ref, xbc_ref, q_ref, k_ref, v_ref, gt_ref, dt_ref):
    x = x_ref[...]
    ms = jnp.mean(x * x, axis=-1, keepdims=True)
    h = (x * lax.rsqrt(ms + NORM_EPS) * g_ref[...]).astype(BF16)
    segs = ((z_ref, SEG_Z, D_INNER), (xbc_ref, SEG_XBC, CONV_DIM), (q_ref, SEG_Q, ATT_WIDTH),
            (gt_ref, SEG_G, 2 * D_MODEL), (dt_ref, SEG_DT, DT_PAD))
    for ref, start, width in segs:
        step = min(width, 512)
        for c in range(0, width, step):
            r = jnp.dot(h, w_ref[:, start + c:start + c + step], preferred_element_type=F32)
            ref[:, c:c + step] = r.astype(ref.dtype)
    tm = x_ref.shape[0]
    for ref, start in ((k_ref, SEG_K), (v_ref, SEG_V)):
        r = jnp.dot(h, w_ref[:, start:start + KV_WIDTH], preferred_element_type=F32)
        for hd in range(N_KV_HEADS):
            ref[pl.ds(hd, tm, stride=N_KV_HEADS), :] = r[:, hd * HEAD_W:(hd + 1) * HEAD_W]


def _inproj(x2d, gain, w_all, tm=512):
    n = x2d.shape[0]
    row = lambda w: pl.BlockSpec((tm, w), lambda i: (i, 0))
    kv_rows = pl.BlockSpec((tm * N_KV_HEADS, HEAD_W), lambda i: (i, 0))
    kv_shape = jax.ShapeDtypeStruct((n * N_KV_HEADS, HEAD_W), F32)
    outs = [(D_INNER, BF16), (CONV_DIM, BF16), (ATT_WIDTH, BF16), None, None, (2 * D_MODEL, BF16), (DT_PAD, F32)]
    return pl.pallas_call(
        _inproj_kernel,
        grid=(n // tm,),
        in_specs=[row(D_MODEL), _resident((1, D_MODEL)), _resident((D_MODEL, W_ALL))],
        out_specs=[kv_rows if o is None else row(o[0]) for o in outs],
        out_shape=[kv_shape if o is None else jax.ShapeDtypeStruct((n, o[0]), o[1]) for o in outs],
        compiler_params=pltpu.CompilerParams(dimension_semantics=("parallel",), vmem_limit_bytes=VMEM_LIMIT),
        name="in_proj",
    )(x2d, gain, w_all)


def _split3_packed(v):
    a = v.astype(BF16).astype(F32)
    r = v - a
    b = r.astype(BF16).astype(F32)
    c = (r - b).astype(BF16).astype(F32)
    packed = a + pltpu.roll(b, N_SSM_HEADS, axis=1) + pltpu.roll(c, 2 * N_SSM_HEADS, axis=1)
    return packed.astype(BF16)


def _ssd_kernel(xbc_ref, dt_ref, pre_ref, st0_ref, cw_ref, cb_ref, dtb_ref, alog_ref, dsk_ref, e64_ref, e128_ref,
                y_ref, conv_ref, st_ref, xp_sc, act_sc, st_sc, cst_sc, *, t_in, n_chunks):
    c = pl.program_id(1)
    T = CHUNK
    row_pad = 8

    @pl.when(c == 0)
    def _():
        xp_sc[...] = jnp.zeros_like(xp_sc)
        xp_sc[row_pad - (D_CONV - 1):row_pad, :] = pre_ref[0]
        for blk in range(D_INNER // LANES):
            st_sc[:, blk * LANES:(blk + 1) * LANES] = st0_ref[0, blk * LANES:(blk + 1) * LANES, :].T

    xp_sc[row_pad:row_pad + t_in, :] = xbc_ref[0].astype(F32)

    cblk = 256
    for j in range(CONV_DIM // cblk):
        cs_ = slice(j * cblk, (j + 1) * cblk)
        window = xp_sc[:, cs_]
        acc = cb_ref[:, cs_] + window[row_pad:] * cw_ref[D_CONV - 1:D_CONV, cs_]
        for k in range(D_CONV - 1):
            shifted = pltpu.roll(window, D_CONV - 1 - k, axis=0)[row_pad:]
            acc = acc + shifted * cw_ref[k:k + 1, cs_]
        act_sc[:, cs_] = _silu(acc)

    lane = lax.broadcasted_iota(jnp.int32, (T, LANES), 1)
    rowi = lax.broadcasted_iota(jnp.int32, (T, LANES), 0)
    if t_in < T:
        dt_raw = jnp.concatenate([dt_ref[0], jnp.zeros((T - t_in, DT_PAD), F32)], axis=0)
    else:
        dt_raw = dt_ref[0]
    v = dt_raw + dtb_ref[...]
    softplus = jnp.maximum(v, 0.0) + jnp.log1p(jnp.exp(-jnp.abs(v)))
    dt = jnp.where((lane < N_SSM_HEADS) & (rowi < t_in), softplus, 0.0)
    la = dt * (-jnp.exp(alog_ref[...]))
    cs = la
    d = 1
    while d < T:
        cs = cs + jnp.where(rowi >= d, pltpu.roll(cs, d, axis=0), 0.0)
        d *= 2
    cst_sc[...] = cs.T
    x_dt = _split3_packed(dt)
    x_cs = _split3_packed(cs)
    causal = rowi >= lane
    lo_half = lane < SSM_HEAD_DIM

    heads_per_group = N_SSM_HEADS // N_SSM_GROUPS
    gw = heads_per_group * SSM_HEAD_DIM
    for g in range(N_SSM_GROUPS):
        bm = act_sc[:, D_INNER + g * D_STATE:D_INNER + (g + 1) * D_STATE]
        cm = act_sc[:, D_INNER + N_SSM_GROUPS * D_STATE + g * D_STATE:
                    D_INNER + N_SSM_GROUPS * D_STATE + (g + 1) * D_STATE]
        bm_b = bm.astype(BF16)
        cm_b = cm.astype(BF16)
        bmt_b = bm.T.astype(BF16)
        cb = lax.dot_general(cm_b, bm_b, (((1,), (1,)), ((), ())), preferred_element_type=F32)
        gs = slice(g * gw, (g + 1) * gw)
        dtx = jnp.dot(x_dt, e64_ref[:, gs], preferred_element_type=F32)
        csx = jnp.dot(x_cs, e64_ref[:, gs], preferred_element_type=F32)
        cs_last = csx[T - 1:T, :]
        xs = act_sc[:, gs]
        xd = xs * dtx
        xdd_b = (xd * jnp.exp(cs_last - csx)).astype(BF16)
        xd_b = xd.astype(BF16)
        st_old = st_sc[:, gs]
        y_off = jnp.dot(cm_b, st_old.astype(BF16), preferred_element_type=F32) * jnp.exp(csx)
        st_sc[:, gs] = st_old * jnp.exp(cs_last) + jnp.dot(bmt_b, xdd_b, preferred_element_type=F32)
        y_skip = xs * dsk_ref[:, gs]
        for pr in range(heads_per_group // 2):
            ms = []
            for hh in range(2):
                h = g * heads_per_group + pr * 2 + hh
                cs_col = jnp.dot(x_cs, e128_ref[:, h * LANES:(h + 1) * LANES], preferred_element_type=F32)
                seg = jnp.where(causal, cs_col - cst_sc[h:h + 1, :], -jnp.inf)
                ms.append((cb * jnp.exp(seg)).astype(BF16))
            ps = slice(pr * LANES, (pr + 1) * LANES)
            xd_p = xd_b[:, ps]
            zero = jnp.zeros_like(xd_p)
            rhs = jnp.concatenate([jnp.where(lo_half, xd_p, zero), jnp.where(lo_half, zero, xd_p)], axis=0)
            y_diag = jnp.dot(jnp.concatenate(ms, axis=1), rhs, preferred_element_type=F32)
            y = y_diag + y_off[:, ps] + y_skip[:, ps]
            y_ref[0, :, g * gw + pr * LANES:g * gw + (pr + 1) * LANES] = y[:t_in].astype(y_ref.dtype)

    @pl.when(c == n_chunks - 1)
    def _():
        conv_ref[0] = xp_sc[row_pad + t_in - (D_CONV - 1):row_pad + t_in, :]
        for blk in range(D_INNER // LANES):
            st_ref[0, blk * LANES:(blk + 1) * LANES, :] = st_sc[:, blk * LANES:(blk + 1) * LANES].T

    if n_chunks > 1:
        xp_sc[0:row_pad, :] = xp_sc[T:T + row_pad, :]


def _expansion_matrices():
    e64 = np.zeros((LANES, D_INNER), np.float32)
    e128 = np.zeros((LANES, N_SSM_HEADS * LANES), np.float32)
    for part in range(3):
        for h in range(N_SSM_HEADS):
            e64[part * N_SSM_HEADS + h, h * SSM_HEAD_DIM:(h + 1) * SSM_HEAD_DIM] = 1.0
            e128[part * N_SSM_HEADS + h, h * LANES:(h + 1) * LANES] = 1.0
    return jnp.asarray(e64, BF16), jnp.asarray(e128, BF16)


def _ssd(xbc, dt, prefix, st0, conv_w, conv_b, dt_bias, a_log, d_skip_x):
    b, l, _ = xbc.shape
    t_in = min(l, CHUNK)
    n_chunks = l // t_in
    e64, e128 = _expansion_matrices()
    per_b = lambda shape: pl.BlockSpec((1,) + shape, lambda i, c: (i, 0, 0))
    per_chunk = lambda w: pl.BlockSpec((1, t_in, w), lambda i, c: (i, c, 0))
    return pl.pallas_call(
        functools.partial(_ssd_kernel, t_in=t_in, n_chunks=n_chunks),
        grid=(b, n_chunks),
        in_specs=[per_chunk(CONV_DIM), per_chunk(DT_PAD), per_b((D_CONV - 1, CONV_DIM)), per_b((D_INNER, D_STATE)),
                  _resident((D_CONV, CONV_DIM)), _resident((1, CONV_DIM)), _resident((1, DT_PAD)),
                  _resident((1, DT_PAD)), _resident((1, D_INNER)), _resident(e64.shape), _resident(e128.shape)],
        out_specs=[per_chunk(D_INNER), per_b((D_CONV - 1, CONV_DIM)), per_b((D_INNER, D_STATE))],
        out_shape=[jax.ShapeDtypeStruct((b, l, D_INNER), BF16),
                   jax.ShapeDtypeStruct((b, D_CONV - 1, CONV_DIM), F32),
                   jax.ShapeDtypeStruct((b, D_INNER, D_STATE), F32)],
        scratch_shapes=[pltpu.VMEM((8 + CHUNK, CONV_DIM), F32), pltpu.VMEM((CHUNK, CONV_DIM), F32),
                        pltpu.VMEM((D_STATE, D_INNER), F32), pltpu.VMEM((LANES, CHUNK), F32)],
        compiler_params=pltpu.CompilerParams(dimension_semantics=("parallel", "arbitrary"),
                                             vmem_limit_bytes=VMEM_LIMIT),
        name="ssd",
    )(xbc, dt, prefix, st0, conv_w, conv_b, dt_bias, a_log, d_skip_x, e64, e128)


def _lambda(lam_ref):
    lv = lam_ref[...]
    s1 = jnp.sum(lv[0:1] * lv[1:2], axis=-1, keepdims=True)
    s2 = jnp.sum(lv[2:3] * lv[3:4], axis=-1, keepdims=True)
    return jnp.exp(s1) - jnp.exp(s2) + LAMBDA_INIT


def _attn_prompt_kernel(lam_ref, q_ref, k_ref, v_ref, sw_ref, o_ref, kall_sc, vtall_sc, qt_sc, m_sc, l_sc, acc_sc,
                        sa_sc, smaxa_sc, sb_sc, smaxb_sc, *, tq, seq):
    h = pl.program_id(1)

    @pl.when(h == 0)
    def _():
        for hd in range(N_KV_HEADS):
            kall_sc[hd] = k_ref[pl.ds(hd, seq, stride=N_KV_HEADS), :].astype(BF16)
            for blk in range(seq // LANES):
                vb = v_ref[pl.ds(hd + blk * LANES * N_KV_HEADS, LANES, stride=N_KV_HEADS), :]
                vtall_sc[hd, :, blk * LANES:(blk + 1) * LANES] = vb.T.astype(BF16)

    k_sc = kall_sc.at[h]
    vt_sc = vtall_sc.at[h]
    lam = _lambda(lam_ref)
    sw = sw_ref[...]
    feat_lo = lax.broadcasted_iota(jnp.int32, (LANES, tq), 0) < ATT_HEAD_DIM

    def kv_slice(start):
        return pl.ds(pl.multiple_of(start, tq), tq)

    def scores(start, stage, causal=None):
        s_ref, smax_ref = stage
        s = jnp.dot(k_sc[kv_slice(start), :], qt_sc[...], preferred_element_type=F32)
        if causal is not None:
            s = jnp.where(causal, s, -jnp.inf)
        s_ref[...] = s
        smax_ref[...] = jnp.max(s, axis=0, keepdims=True)

    def consume(start, stage):
        s_ref, smax_ref = stage
        m_prev = m_sc[...]
        m_new = jnp.maximum(m_prev, smax_ref[...])
        alpha = jnp.exp2(m_prev - m_new)
        p = jnp.exp2(s_ref[...] - m_new)
        l_sc[...] = alpha * l_sc[...] + jnp.sum(p, axis=0, keepdims=True)
        pv = jnp.dot(vt_sc[:, kv_slice(start)], p.astype(BF16), preferred_element_type=F32)
        acc_sc[...] = alpha * acc_sc[...] + pv
        m_sc[...] = m_new

    stage_a, stage_b = (sa_sc, smaxa_sc), (sb_sc, smaxb_sc)

    def q_block(qi, carry):
        qs = pl.ds(pl.multiple_of(qi * tq, tq), tq)
        for r in range(2):
            pair_t = q_ref[0, qs, r * LANES:(r + 1) * LANES].astype(F32).T
            zero = jnp.zeros_like(pair_t)
            qt_sc[:, (2 * r) * tq:(2 * r + 1) * tq] = jnp.where(feat_lo, pair_t, zero).astype(BF16)
            qt_sc[:, (2 * r + 1) * tq:(2 * r + 2) * tq] = jnp.where(feat_lo, zero, pair_t).astype(BF16)
        m_sc[...] = jnp.full_like(m_sc, -jnp.inf)
        l_sc[...] = jnp.zeros_like(l_sc)
        acc_sc[...] = jnp.zeros_like(acc_sc)

        key_i = lax.broadcasted_iota(jnp.int32, (tq, 4 * tq), 0)
        qt_i = lax.broadcasted_iota(jnp.int32, (tq, 4 * tq), 1) & (tq - 1)
        scores(qi * tq, stage_a, causal=key_i <= qt_i)

        def kv_pair(t, start_in_a):
            scores(2 * t * tq, stage_b)
            consume(start_in_a, stage_a)
            scores((2 * t + 1) * tq, stage_a)
            consume(2 * t * tq, stage_b)
            return (2 * t + 1) * tq

        start_in_a = lax.fori_loop(0, qi // 2, kv_pair, qi * tq)

        @pl.when(qi % 2 == 1)
        def _():
            scores((qi - 1) * tq, stage_b)
            consume(start_in_a, stage_a)
            consume((qi - 1) * tq, stage_b)

        @pl.when(qi % 2 == 0)
        def _():
            consume(start_in_a, stage_a)

        on = acc_sc[...] / l_sc[...]
        for r in range(2):
            o_t = on[:, (2 * r) * tq:(2 * r + 1) * tq] - lam * on[:, (2 * r + 1) * tq:(2 * r + 2) * tq]
            o = o_t.T
            ms = jnp.mean(o * o, axis=-1, keepdims=True)
            o = o * lax.rsqrt(ms + SUBLN_EPS) * sw * (1.0 - LAMBDA_INIT)
            o_ref[0, qs, r * LANES:(r + 1) * LANES] = o.astype(o_ref.dtype)
        return carry

    lax.fori_loop(0, seq // tq, q_block, 0)


def _attn_prompt(q, k, v, lam_vecs, subln_w, tq=256):
    b, seq, _ = q.shape
    kv_spec = pl.BlockSpec((None, seq * N_KV_HEADS, HEAD_W), lambda i, h: (i, 0, 0))
    return pl.pallas_call(
        functools.partial(_attn_prompt_kernel, tq=tq, seq=seq),
        grid=(b, N_KV_HEADS),
        in_specs=[_resident(lam_vecs.shape),
                  pl.BlockSpec((1, seq, 2 * HEAD_W), lambda i, h: (i, 0, h)),
                  kv_spec, kv_spec, _resident((1, HEAD_W))],
        out_specs=pl.BlockSpec((1, seq, 2 * HEAD_W), lambda i, h: (i, 0, h)),
        out_shape=jax.ShapeDtypeStruct((b, seq, ATT_WIDTH), BF16),
        scratch_shapes=[pltpu.VMEM((N_KV_HEADS, seq, HEAD_W), BF16), pltpu.VMEM((N_KV_HEADS, HEAD_W, seq), BF16),
                        pltpu.VMEM((HEAD_W, 4 * tq), BF16), pltpu.VMEM((1, 4 * tq), F32),
                        pltpu.VMEM((1, 4 * tq), F32), pltpu.VMEM((HEAD_W, 4 * tq), F32),
                        pltpu.VMEM((tq, 4 * tq), F32), pltpu.VMEM((1, 4 * tq), F32),
                        pltpu.VMEM((tq, 4 * tq), F32), pltpu.VMEM((1, 4 * tq), F32)],
        compiler_params=pltpu.CompilerParams(dimension_semantics=("parallel", "arbitrary"),
                                             vmem_limit_bytes=VMEM_LIMIT),
        name="attn_prompt",
    )(lam_vecs, q, k, v, subln_w)


def _col_broadcast(row):
    return jnp.broadcast_to(row, (LANES, LANES)).T


def _attn_paged_kernel(pt_ref, lam_ref, q_ref, kn_ref, vn_ref, sw_ref, *rest, pages_per_step, n_steps, t_new):
    k_pages = rest[:pages_per_step]
    v_pages = rest[pages_per_step:2 * pages_per_step]
    o_ref, qall_sc, sa_sc, smaxa_sc, sb_sc, smaxb_sc, m_sc, l_sc, acc_sc = rest[2 * pages_per_step:]
    step = pl.program_id(1)
    rows_pg = PAGE_SIZE * N_KV_HEADS
    head_shift = int(math.log2(N_KV_HEADS))
    col_shift = int(math.log2(4 * t_new))
    row_i = lax.broadcasted_iota(jnp.int32, (rows_pg, LANES), 0)
    col_i = lax.broadcasted_iota(jnp.int32, (rows_pg, LANES), 1)
    head_match = (row_i & (N_KV_HEADS - 1)) == (col_i >> col_shift)

    @pl.when(step == 0)
    def _():
        lane_lo = lax.broadcasted_iota(jnp.int32, (t_new, LANES), 1) < ATT_HEAD_DIM
        pieces = []
        for hr in range(2 * N_KV_HEADS):
            pair = q_ref[0, :, hr * HEAD_W:(hr + 1) * HEAD_W].astype(F32)
            zero = jnp.zeros_like(pair)
            pieces += [jnp.where(lane_lo, pair, zero), jnp.where(lane_lo, zero, pair)]
        qall_sc[...] = jnp.concatenate(pieces, axis=0).T.astype(BF16)
        m_sc[...] = jnp.full_like(m_sc, -jnp.inf)
        l_sc[...] = jnp.zeros_like(l_sc)
        acc_sc[...] = jnp.zeros_like(acc_sc)

    def stage_scores(stage, n_blocks, rows, score_fn):
        s_ref, smax_ref = stage
        smax = jnp.full((1, LANES), -jnp.inf, F32)
        for i in range(n_blocks):
            s = score_fn(i)
            s_ref[i * rows_pg:i * rows_pg + rows, :] = s
            smax = jnp.maximum(smax, jnp.max(s, axis=0, keepdims=True))
        smax_ref[...] = smax

    def consume(stage, n_blocks, rows, value_fn):
        s_ref, smax_ref = stage
        m_prev = m_sc[...]
        m_new = jnp.maximum(m_prev, smax_ref[...])
        alpha = jnp.exp2(m_prev - m_new)
        lsum = jnp.zeros_like(m_new)
        pv = jnp.zeros((LANES, HEAD_W), F32)
        for i in range(n_blocks):
            p = jnp.exp2(s_ref[i * rows_pg:i * rows_pg + rows, :] - m_new)
            lsum = lsum + jnp.sum(p, axis=0, keepdims=True)
            pv = pv + jnp.dot(p.T.astype(BF16), value_fn(i), preferred_element_type=F32)
        l_sc[...] = alpha * l_sc[...] + lsum
        acc_sc[...] = _col_broadcast(alpha) * acc_sc[...] + pv
        m_sc[...] = m_new

    def page_scores(i):
        s = jnp.dot(k_pages[i][0].astype(BF16), qall_sc[...], preferred_element_type=F32)
        return jnp.where(head_match, s, -jnp.inf)

    page_values = lambda i: v_pages[i][0].astype(BF16)
    stages = ((sa_sc, smaxa_sc), (sb_sc, smaxb_sc))

    @pl.when(step == 0)
    def _():
        stage_scores(stages[0], pages_per_step, rows_pg, page_scores)

    for parity in range(2):
        @pl.when((step >= 1) & (step < n_steps) & (step % 2 == parity))
        def _():
            stage_scores(stages[parity], pages_per_step, rows_pg, page_scores)
            consume(stages[1 - parity], pages_per_step, rows_pg, page_values)

    @pl.when(step == n_steps)
    def _():
        consume(stages[(n_steps - 1) % 2], pages_per_step, rows_pg, page_values)
        pad = jnp.zeros((LANES - t_new * N_KV_HEADS, HEAD_W), F32)
        kn = jnp.concatenate([kn_ref[0], pad], axis=0).astype(BF16)
        vn = jnp.concatenate([vn_ref[0], pad], axis=0).astype(BF16)
        row_n = lax.broadcasted_iota(jnp.int32, (LANES, LANES), 0)
        col_n = lax.broadcasted_iota(jnp.int32, (LANES, LANES), 1)
        live = (((row_n & (N_KV_HEADS - 1)) == (col_n >> col_shift))
                & ((row_n >> head_shift) <= (col_n & (t_new - 1))))
        s_new = jnp.where(live, jnp.dot(kn, qall_sc[...], preferred_element_type=F32), -jnp.inf)
        stage_scores(stages[0], 1, LANES, lambda i: s_new)
        consume(stages[0], 1, LANES, lambda i: vn)
        lam = _lambda(lam_ref)
        sw = sw_ref[...]
        on = acc_sc[...] / _col_broadcast(l_sc[...])
        for hr in range(2 * N_KV_HEADS):
            base = hr * 2 * t_new
            o = on[base:base + t_new, :] - lam * on[base + t_new:base + 2 * t_new, :]
            ms = jnp.mean(o * o, axis=-1, keepdims=True)
            o = o * lax.rsqrt(ms + SUBLN_EPS) * sw * (1.0 - LAMBDA_INIT)
            o_ref[0, :, hr * HEAD_W:(hr + 1) * HEAD_W] = o.astype(o_ref.dtype)


def _attn_paged(q, k_new, v_new, cache_k, cache_v, page_table, lam_vecs, subln_w, pages_per_step=8):
    b, t_new, _ = q.shape
    n_pages = page_table.shape[1]
    n_steps = n_pages // pages_per_step
    rows_pg = PAGE_SIZE * N_KV_HEADS
    assert N_KV_HEADS * 4 * t_new == LANES
    pt_flat = page_table.reshape(-1)

    def page_spec(i, lag):
        def index(bi, s, pt):
            group = jnp.clip(s - lag, 0, n_steps - 1)
            return (pt[bi * n_pages + group * pages_per_step + i], 0, 0)
        return pl.BlockSpec((1, rows_pg, HEAD_W), index)

    per_b = lambda r, w: pl.BlockSpec((1, r, w), lambda bi, s, pt: (bi, 0, 0))
    const = lambda shape: pl.BlockSpec(shape, lambda bi, s, pt: (0,) * len(shape))
    stage_rows = pages_per_step * rows_pg
    grid_spec = pltpu.PrefetchScalarGridSpec(
        num_scalar_prefetch=1,
        grid=(b, n_steps + 1),
        in_specs=[const(lam_vecs.shape), per_b(t_new, ATT_WIDTH), per_b(t_new * N_KV_HEADS, HEAD_W),
                  per_b(t_new * N_KV_HEADS, HEAD_W), const((1, HEAD_W))]
                 + [page_spec(i, 0) for i in range(pages_per_step)]
                 + [page_spec(i, 1) for i in range(pages_per_step)],
        out_specs=per_b(t_new, ATT_WIDTH),
        scratch_shapes=[pltpu.VMEM((HEAD_W, LANES), BF16),
                        pltpu.VMEM((stage_rows, LANES), F32), pltpu.VMEM((1, LANES), F32),
                        pltpu.VMEM((stage_rows, LANES), F32), pltpu.VMEM((1, LANES), F32),
                        pltpu.VMEM((1, LANES), F32), pltpu.VMEM((1, LANES), F32), pltpu.VMEM((LANES, HEAD_W), F32)],
    )
    return pl.pallas_call(
        functools.partial(_attn_paged_kernel, pages_per_step=pages_per_step, n_steps=n_steps, t_new=t_new),
        grid_spec=grid_spec,
        out_shape=jax.ShapeDtypeStruct((b, t_new, ATT_WIDTH), BF16),
        compiler_params=pltpu.CompilerParams(dimension_semantics=("parallel", "arbitrary"),
                                             vmem_limit_bytes=VMEM_LIMIT),
        name="attn_paged",
    )(pt_flat, lam_vecs, q, k_new, v_new, subln_w, *([cache_k] * pages_per_step), *([cache_v] * pages_per_step))


def _mix_kernel(y_ref, z_ref, o_ref, g_ref, x_ref, nw_ref, wso_ref, wao_ref, wo_ref, out_ref):
    y = y_ref[...].astype(F32) * _silu(z_ref[...].astype(F32))
    gw = D_INNER // N_SSM_GROUPS
    parts = []
    for g in range(N_SSM_GROUPS):
        yg = y[:, g * gw:(g + 1) * gw]
        ms = jnp.mean(yg * yg, axis=-1, keepdims=True)
        parts.append((yg * lax.rsqrt(ms + NORM_EPS) * nw_ref[:, g * gw:(g + 1) * gw]).astype(BF16))
    y_ssm = jnp.dot(jnp.concatenate(parts, axis=1), wso_ref[...], preferred_element_type=F32)
    y_att = jnp.dot(o_ref[...], wao_ref[...], preferred_element_type=F32)
    gates = jax.nn.sigmoid(g_ref[...].astype(F32))
    mixed = (gates[:, :D_MODEL] * y_ssm + gates[:, D_MODEL:] * y_att).astype(BF16)
    out_ref[...] = x_ref[...] + jnp.dot(mixed, wo_ref[...], preferred_element_type=F32)


def _mix(y, z, o, gates, x, ssm_norm_w, w_ssm_out, w_att_out, w_o, tm=512):
    n = x.shape[0]
    row = lambda w: pl.BlockSpec((tm, w), lambda i: (i, 0))
    return pl.pallas_call(
        _mix_kernel,
        grid=(n // tm,),
        in_specs=[row(D_INNER), row(D_INNER), row(ATT_WIDTH), row(2 * D_MODEL), row(D_MODEL),
                  _resident((1, D_INNER)), _resident((D_INNER, D_MODEL)), _resident((ATT_WIDTH, D_MODEL)),
                  _resident((D_MODEL, D_MODEL))],
        out_specs=row(D_MODEL),
        out_shape=jax.ShapeDtypeStruct((n, D_MODEL), F32),
        compiler_params=pltpu.CompilerParams(dimension_semantics=("parallel",), vmem_limit_bytes=VMEM_LIMIT),
        name="mix",
    )(y, z, o, gates, x, ssm_norm_w, w_ssm_out, w_att_out, w_o)


def _ffn_kernel(x_ref, nw_ref, win_ref, wout_ref, fw_ref, out_ref, *, chunk):
    x = x_ref[...]
    ms = jnp.mean(x * x, axis=-1, keepdims=True)
    h = (x * lax.rsqrt(ms + NORM_EPS) * nw_ref[...]).astype(BF16)
    acc = x
    for c in range(0, D_FF, chunk):
        gate = jnp.dot(h, win_ref[:, c:c + chunk], preferred_element_type=F32)
        up = jnp.dot(h, win_ref[:, D_FF + c:D_FF + c + chunk], preferred_element_type=F32)
        hid = (_silu(gate) * up).astype(BF16)
        acc = acc + jnp.dot(hid, wout_ref[c:c + chunk, :], preferred_element_type=F32)
    ms2 = jnp.mean(acc * acc, axis=-1, keepdims=True)
    out_ref[...] = acc * lax.rsqrt(ms2 + NORM_EPS) * fw_ref[...]


def _ffn(x, norm_w, w_in, w_out, final_w, tm=512, chunk=256):
    n = x.shape[0]
    row = pl.BlockSpec((tm, D_MODEL), lambda i: (i, 0))
    return pl.pallas_call(
        functools.partial(_ffn_kernel, chunk=chunk),
        grid=(n // tm,),
        in_specs=[row, _resident((1, D_MODEL)), _resident((D_MODEL, 2 * D_FF)), _resident((D_FF, D_MODEL)),
                  _resident((1, D_MODEL))],
        out_specs=row,
        out_shape=jax.ShapeDtypeStruct((n, D_MODEL), F32),
        compiler_params=pltpu.CompilerParams(dimension_semantics=("parallel",), vmem_limit_bytes=VMEM_LIMIT),
        name="ffn",
    )(x, norm_w, w_in, w_out, final_w)


def _pack_w_in(w_in):
    cuts = np.cumsum((D_INNER, CONV_DIM, N_SSM_HEADS, ATT_WIDTH, KV_WIDTH, KV_WIDTH, 2 * D_MODEL))[:-1]
    wz, wxbc, wdt, wq, wk, wv, wg = jnp.split(w_in, [int(c) for c in cuts], axis=-1)
    wdt = jnp.pad(wdt, ((0, 0), (0, DT_PAD - N_SSM_HEADS)))
    wq = wq * (ATT_HEAD_DIM ** -0.5 * math.log2(math.e))
    return jnp.concatenate([wz, wxbc, wq, wk, wv, wg, wdt], axis=-1).astype(BF16)


def kernel(x_prompt, x_sample, cache_k, cache_v, state_conv, state_ssm, page_table, norm_mix_w, w_in, conv_w, conv_b, dt_bias, a_log, d_skip, ssm_norm_w, w_ssm_out, lambda_q1, lambda_k1, lambda_q2, lambda_k2, subln_w, w_att_out, w_o, norm_ffn_w, w_ffn_in, w_ffn_out, norm_final_w):
    l = 0
    w_all = _pack_w_in(w_in[l])
    pad_heads = lambda t: jnp.pad(t.reshape(1, N_SSM_HEADS), ((0, 0), (0, DT_PAD - N_SSM_HEADS)))
    lam_vecs = jnp.stack([lambda_q1[l], lambda_k1[l], lambda_q2[l], lambda_k2[l]])
    params = dict(
        gain=norm_mix_w[l].reshape(1, D_MODEL), conv_w=conv_w[l], conv_b=conv_b[l].reshape(1, CONV_DIM),
        dt_bias=pad_heads(dt_bias[l]), a_log=pad_heads(a_log[l]),
        d_skip_x=jnp.repeat(d_skip[l], SSM_HEAD_DIM).reshape(1, D_INNER),
        ssm_norm_w=ssm_norm_w[l].reshape(1, D_INNER), w_ssm_out=w_ssm_out[l].astype(BF16),
        subln_w=subln_w[l].reshape(1, 2 * ATT_HEAD_DIM), w_att_out=w_att_out[l].astype(BF16),
        w_o=w_o[l].astype(BF16), norm_ffn_w=norm_ffn_w[l].reshape(1, D_MODEL),
        w_ffn_in=w_ffn_in[l].astype(BF16), w_ffn_out=w_ffn_out[l].astype(BF16),
        norm_final_w=norm_final_w.reshape(1, D_MODEL))

    def group(x, conv0, h0, attend):
        b, seq, _ = x.shape
        n = b * seq
        x2 = x.reshape(n, D_MODEL)
        z, xbc, q, k, v, gates, dt = _inproj(x2, params["gain"], w_all)
        y, conv_state, ssm_state = _ssd(
            xbc.reshape(b, seq, CONV_DIM), dt.reshape(b, seq, DT_PAD), conv0, h0.reshape(b, D_INNER, D_STATE),
            params["conv_w"], params["conv_b"], params["dt_bias"], params["a_log"], params["d_skip_x"])
        k3 = k.reshape(b, seq * N_KV_HEADS, HEAD_W)
        v3 = v.reshape(b, seq * N_KV_HEADS, HEAD_W)
        o = attend(q.reshape(b, seq, ATT_WIDTH), k3, v3)
        x1 = _mix(y.reshape(n, D_INNER), z, o.reshape(n, ATT_WIDTH), gates, x2, params["ssm_norm_w"],
                  params["w_ssm_out"], params["w_att_out"], params["w_o"])
        out = _ffn(x1, params["norm_ffn_w"], params["w_ffn_in"], params["w_ffn_out"], params["norm_final_w"])
        kv_shape = (1, b, seq, N_KV_HEADS, 2 * ATT_HEAD_DIM)
        return (out.reshape(b, seq, D_MODEL), k.reshape(kv_shape), v.reshape(kv_shape), conv_state[None],
                ssm_state.reshape(1, b, N_SSM_HEADS, SSM_HEAD_DIM, D_STATE))

    bp = x_prompt.shape[0]
    attend_p = lambda q, k, v: _attn_prompt(q, k, v, lam_vecs, params["subln_w"])
    yp, kp, vp, cp, hp = group(x_prompt, jnp.zeros((bp, D_CONV - 1, CONV_DIM), F32),
                               jnp.zeros((bp, N_SSM_HEADS, SSM_HEAD_DIM, D_STATE), F32), attend_p)
    n_pool = cache_k.shape[1]
    ck = cache_k[l].reshape(n_pool, PAGE_SIZE * N_KV_HEADS, HEAD_W)
    cv = cache_v[l].reshape(n_pool, PAGE_SIZE * N_KV_HEADS, HEAD_W)
    attend_s = lambda q, k, v: _attn_paged(q, k, v, ck, cv, page_table, lam_vecs, params["subln_w"])
    ys, ks, vs, cs, hs = group(x_sample, state_conv[l], state_ssm[l], attend_s)
    return (yp, ys, kp, vp, cp, hp, ks, vs, cs, hs)
```

```python
import functools
import math

import jax
import jax.numpy as jnp
import numpy as np
from jax import lax
from jax.experimental import pallas as pl
from jax.experimental.pallas import tpu as pltpu

F32 = jnp.float32
BF16 = jnp.bfloat16

D_MODEL = 1024
D_INNER = 2048
SSM_HEAD_DIM = 64
N_SSM_HEADS = 32
N_SSM_GROUPS = 4
D_STATE = 128
D_CONV = 4
CONV_DIM = 3072
ATT_HEAD_DIM = 64
N_KV_HEADS = 4
ATT_WIDTH = 1024
KV_WIDTH = 512
D_FF = 2816
PAGE_SIZE = 128
HEAD_W = 2 * ATT_HEAD_DIM
NORM_EPS = 1e-6
SUBLN_EPS = 1e-5
LAMBDA_INIT = 0.8 - 0.6 * math.exp(-0.3 * 0)

LANES = 128
bf16_tile_rows = 16
CHUNK = 128
DT_PAD = LANES
VMEM_LIMIT = 56 * 1024 * 1024

SEG_Z, SEG_XBC, SEG_Q, SEG_K, SEG_V, SEG_G, SEG_DT = 0, 2048, 5120, 6144, 6656, 7168, 9216
W_ALL = SEG_DT + DT_PAD


def _resident(shape):
    nd = len(shape)
    return pl.BlockSpec(shape, lambda *_: (0,) * nd, pipeline_mode=pl.Buffered(1))


def _silu(x):
    return x * jax.nn.sigmoid(x)


def _silu_tanh(x):
    h = 0.5 * x
    return h + h * jnp.tanh(h)


def _inproj_kernel(x_ref, g_ref, w_ref, z_ref, xbc_ref, q_ref, k_ref, v_ref, gt_ref, dt_ref):
    x = x_ref[...]
    ms = jnp.mean(x * x, axis=-1, keepdims=True)
    h = (x * lax.rsqrt(ms + NORM_EPS) * g_ref[...]).astype(BF16)
    segs = ((z_ref, SEG_Z, D_INNER), (xbc_ref, SEG_XBC, CONV_DIM), (q_ref, SEG_Q, ATT_WIDTH),
            (gt_ref, SEG_G, 2 * D_MODEL), (dt_ref, SEG_DT, DT_PAD))
    for ref, start, width in segs:
        step = min(width, 512)
        for c in range(0, width, step):
            r = jnp.dot(h, w_ref[:, start + c:start + c + step], preferred_element_type=F32)
            ref[:, c:c + step] = r.astype(ref.dtype)
    tm = x_ref.shape[0]
    for ref, start in ((k_ref, SEG_K), (v_ref, SEG_V)):
        r = jnp.dot(h, w_ref[:, start:start + KV_WIDTH], preferred_element_type=F32)
        for hd in range(N_KV_HEADS):
            ref[pl.ds(hd, tm, stride=N_KV_HEADS), :] = r[:, hd * HEAD_W:(hd + 1) * HEAD_W]


def _inproj(x2d, gain, w_all, tm=512):
    n = x2d.shape[0]
    row = lambda w: pl.BlockSpec((tm, w), lambda i: (i, 0))
    kv_rows = pl.BlockSpec((tm * N_KV_HEADS, HEAD_W), lambda i: (i, 0))
    kv_shape = jax.ShapeDtypeStruct((n * N_KV_HEADS, HEAD_W), F32)
    outs = [(D_INNER, BF16), (CONV_DIM, BF16), (ATT_WIDTH, BF16), None, None, (2 * D_MODEL, BF16), (DT_PAD, F32)]
    return pl.pallas_call(
        _inproj_kernel,
        grid=(n // tm,),
        in_specs=[row(D_MODEL), _resident((1, D_MODEL)), _resident((D_MODEL, W_ALL))],
        out_specs=[kv_rows if o is None else row(o[0]) for o in outs],
        out_shape=[kv_shape if o is None else jax.ShapeDtypeStruct((n, o[0]), o[1]) for o in outs],
        compiler_params=pltpu.CompilerParams(dimension_semantics=("parallel",), vmem_limit_bytes=VMEM_LIMIT),
        name="in_proj",
    )(x2d, gain, w_all)


def _split3_packed(v):
    a = v.astype(BF16).astype(F32)
    r = v - a
    b = r.astype(BF16).astype(F32)
    c = (r - b).astype(BF16).astype(F32)
    packed = a + pltpu.roll(b, N_SSM_HEADS, axis=1) + pltpu.roll(c, 2 * N_SSM_HEADS, axis=1)
    return packed.astype(BF16)


def _ssd_kernel(xbc_ref, dt_ref, pre_ref, st0_ref, cw_ref, cb_ref, dtb_ref, alog_ref, dsk_ref, e64_ref, e128_ref,
                y_ref, conv_ref, st_ref, xp_sc, act_sc, st_sc, cst_sc, *, t_in, n_chunks):
    c = pl.program_id(1)
    T = CHUNK
    row_pad = 8

    @pl.when(c == 0)
    def _():
        xp_sc[...] = jnp.zeros_like(xp_sc)
        xp_sc[row_pad - (D_CONV - 1):row_pad, :] = pre_ref[0]
        for blk in range(D_INNER // LANES):
            st_sc[:, blk * LANES:(blk + 1) * LANES] = st0_ref[0, blk * LANES:(blk + 1) * LANES, :].T

    xp_sc[row_pad:row_pad + t_in, :] = xbc_ref[0].astype(F32)

    cblk = 256
    for j in range(CONV_DIM // cblk):
        cs_ = slice(j * cblk, (j + 1) * cblk)
        window = xp_sc[:, cs_]
        acc = cb_ref[:, cs_] + window[row_pad:] * cw_ref[D_CONV - 1:D_CONV, cs_]
        for k in range(D_CONV - 1):
            shifted = pltpu.roll(window, D_CONV - 1 - k, axis=0)[row_pad:]
            acc = acc + shifted * cw_ref[k:k + 1, cs_]
        act_sc[:, cs_] = _silu_tanh(acc)

    lane = lax.broadcasted_iota(jnp.int32, (T, LANES), 1)
    rowi = lax.broadcasted_iota(jnp.int32, (T, LANES), 0)
    if t_in < T:
        dt_raw = jnp.concatenate([dt_ref[0], jnp.zeros((T - t_in, DT_PAD), F32)], axis=0)
    else:
        dt_raw = dt_ref[0]
    v = dt_raw + dtb_ref[...]
    softplus = jnp.maximum(v, 0.0) + jnp.log1p(jnp.exp(-jnp.abs(v)))
    dt = jnp.where((lane < N_SSM_HEADS) & (rowi < t_in), softplus, 0.0)
    la = dt * (-jnp.exp(alog_ref[...]))
    cs = la
    d = 1
    while d < T:
        cs = cs + jnp.where(rowi >= d, pltpu.roll(cs, d, axis=0), 0.0)
        d *= 2
    cs = cs * math.log2(math.e)
    cst_sc[...] = cs.T
    x_dt = _split3_packed(dt)
    x_cs = _split3_packed(cs)
    causal = rowi >= lane
    lo_half = lane < SSM_HEAD_DIM

    heads_per_group = N_SSM_HEADS // N_SSM_GROUPS
    gw = heads_per_group * SSM_HEAD_DIM
    for g in range(N_SSM_GROUPS):
        bm = act_sc[:, D_INNER + g * D_STATE:D_INNER + (g + 1) * D_STATE]
        cm = act_sc[:, D_INNER + N_SSM_GROUPS * D_STATE + g * D_STATE:
                    D_INNER + N_SSM_GROUPS * D_STATE + (g + 1) * D_STATE]
        bm_b = bm.astype(BF16)
        cm_b = cm.astype(BF16)
        bmt_b = bm.T.astype(BF16)
        cb = lax.dot_general(cm_b, bm_b, (((1,), (1,)), ((), ())), preferred_element_type=F32)
        gs = slice(g * gw, (g + 1) * gw)
        dtx = jnp.dot(x_dt, e64_ref[:, gs], preferred_element_type=F32)
        csx = jnp.dot(x_cs, e64_ref[:, gs], preferred_element_type=F32)
        cs_last = csx[T - 1:T, :]
        xs = act_sc[:, gs]
        xd = xs * dtx
        xdd_b = (xd * jnp.exp2(cs_last - csx)).astype(BF16)
        xd_b = xd.astype(BF16)
        st_old = st_sc[:, gs]
        y_off = jnp.dot(cm_b, st_old.astype(BF16), preferred_element_type=F32) * jnp.exp2(csx)
        st_sc[:, gs] = st_old * jnp.exp2(cs_last) + jnp.dot(bmt_b, xdd_b, preferred_element_type=F32)
        y_skip = xs * dsk_ref[:, gs]
        for pr in range(heads_per_group // 2):
            ms = []
            for hh in range(2):
                h = g * heads_per_group + pr * 2 + hh
                cs_col = jnp.dot(x_cs, e128_ref[:, h * LANES:(h + 1) * LANES], preferred_element_type=F32)
                seg = jnp.where(causal, cs_col - cst_sc[h:h + 1, :], -jnp.inf)
                ms.append((cb * jnp.exp2(seg)).astype(BF16))
            ps = slice(pr * LANES, (pr + 1) * LANES)
            xd_p = xd_b[:, ps]
            zero = jnp.zeros_like(xd_p)
            rhs = jnp.concatenate([jnp.where(lo_half, xd_p, zero), jnp.where(lo_half, zero, xd_p)], axis=0)
            y_diag = jnp.dot(jnp.concatenate(ms, axis=1), rhs, preferred_element_type=F32)
            y = y_diag + y_off[:, ps] + y_skip[:, ps]
            y_ref[0, :, g * gw + pr * LANES:g * gw + (pr + 1) * LANES] = y[:t_in].astype(y_ref.dtype)

    @pl.when(c == n_chunks - 1)
    def _():
        conv_ref[0] = xp_sc[row_pad + t_in - (D_CONV - 1):row_pad + t_in, :]
        for blk in range(D_INNER // LANES):
            st_ref[0, blk * LANES:(blk + 1) * LANES, :] = st_sc[:, blk * LANES:(blk + 1) * LANES].T

    if n_chunks > 1:
        xp_sc[0:row_pad, :] = xp_sc[T:T + row_pad, :]


def _expansion_matrices():
    e64 = np.zeros((LANES, D_INNER), np.float32)
    e128 = np.zeros((LANES, N_SSM_HEADS * LANES), np.float32)
    for part in range(3):
        for h in range(N_SSM_HEADS):
            e64[part * N_SSM_HEADS + h, h * SSM_HEAD_DIM:(h + 1) * SSM_HEAD_DIM] = 1.0
            e128[part * N_SSM_HEADS + h, h * LANES:(h + 1) * LANES] = 1.0
    return jnp.asarray(e64, BF16), jnp.asarray(e128, BF16)


def _ssd(xbc, dt, prefix, st0, conv_w, conv_b, dt_bias, a_log, d_skip_x):
    b, l, _ = xbc.shape
    t_in = min(l, CHUNK)
    n_chunks = l // t_in
    e64, e128 = _expansion_matrices()
    per_b = lambda shape: pl.BlockSpec((1,) + shape, lambda i, c: (i, 0, 0))
    per_chunk = lambda w: pl.BlockSpec((1, t_in, w), lambda i, c: (i, c, 0))
    return pl.pallas_call(
        functools.partial(_ssd_kernel, t_in=t_in, n_chunks=n_chunks),
        grid=(b, n_chunks),
        in_specs=[per_chunk(CONV_DIM), per_chunk(DT_PAD), per_b((D_CONV - 1, CONV_DIM)), per_b((D_INNER, D_STATE)),
                  _resident((D_CONV, CONV_DIM)), _resident((1, CONV_DIM)), _resident((1, DT_PAD)),
                  _resident((1, DT_PAD)), _resident((1, D_INNER)), _resident(e64.shape), _resident(e128.shape)],
        out_specs=[per_chunk(D_INNER), per_b((D_CONV - 1, CONV_DIM)), per_b((D_INNER, D_STATE))],
        out_shape=[jax.ShapeDtypeStruct((b, l, D_INNER), BF16),
                   jax.ShapeDtypeStruct((b, D_CONV - 1, CONV_DIM), F32),
                   jax.ShapeDtypeStruct((b, D_INNER, D_STATE), F32)],
        scratch_shapes=[pltpu.VMEM((8 + CHUNK, CONV_DIM), F32), pltpu.VMEM((CHUNK, CONV_DIM), F32),
                        pltpu.VMEM((D_STATE, D_INNER), F32), pltpu.VMEM((LANES, CHUNK), F32)],
        compiler_params=pltpu.CompilerParams(dimension_semantics=("parallel", "arbitrary"),
                                             vmem_limit_bytes=VMEM_LIMIT),
        name="ssd",
    )(xbc, dt, prefix, st0, conv_w, conv_b, dt_bias, a_log, d_skip_x, e64, e128)


def _lambda(lam_ref):
    lv = lam_ref[...]
    s1 = jnp.sum(lv[0:1] * lv[1:2], axis=-1, keepdims=True)
    s2 = jnp.sum(lv[2:3] * lv[3:4], axis=-1, keepdims=True)
    return jnp.exp(s1) - jnp.exp(s2) + LAMBDA_INIT


def _attn_prompt_kernel(lam_ref, q_ref, k_ref, v_ref, sw_ref, o_ref, kall_sc, vtall_sc, qt_sc, m_sc, acc_sc,
                        sa_sc, smaxa_sc, sb_sc, smaxb_sc, *, tq, seq):
    h = pl.program_id(1)
    extra = vtall_sc.shape[1] - HEAD_W

    @pl.when(h == 0)
    def _():
        ones_row = (lax.broadcasted_iota(jnp.int32, (extra, seq), 0) == 0).astype(F32).astype(BF16)
        for hd in range(N_KV_HEADS):
            kall_sc[hd] = k_ref[pl.ds(hd, seq, stride=N_KV_HEADS), :].astype(BF16)
            for blk in range(seq // LANES):
                vb = v_ref[pl.ds(hd + blk * LANES * N_KV_HEADS, LANES, stride=N_KV_HEADS), :]
                vtall_sc[hd, 0:HEAD_W, blk * LANES:(blk + 1) * LANES] = vb.T.astype(BF16)
            vtall_sc[hd, HEAD_W:, :] = ones_row

    k_sc = kall_sc.at[h]
    vt_sc = vtall_sc.at[h]
    lam = _lambda(lam_ref)
    sw = sw_ref[...]
    feat_lo = lax.broadcasted_iota(jnp.int32, (LANES, tq), 0) < ATT_HEAD_DIM

    def kv_slice(start):
        return pl.ds(pl.multiple_of(start, tq), tq)

    def scores(start, stage, causal=None):
        s_ref, smax_ref = stage
        s = jnp.dot(k_sc[kv_slice(start), :], qt_sc[...], preferred_element_type=F32)
        if causal is not None:
            s = jnp.where(causal, s, -jnp.inf)
        s_ref[...] = s
        smax_ref[...] = jnp.max(s, axis=0, keepdims=True)

    def consume(start, stage):
        s_ref, smax_ref = stage
        m_prev = m_sc[...]
        m_new = jnp.maximum(m_prev, smax_ref[...])
        alpha = jnp.exp2(m_prev - m_new)
        p = jnp.exp2(s_ref[...] - m_new)
        pv = jnp.dot(vt_sc[:, kv_slice(start)], p.astype(BF16), preferred_element_type=F32)
        acc_sc[...] = alpha * acc_sc[...] + pv
        m_sc[...] = m_new

    stage_a, stage_b = (sa_sc, smaxa_sc), (sb_sc, smaxb_sc)

    def q_block(qi, carry):
        qs = pl.ds(pl.multiple_of(qi * tq, tq), tq)
        for r in range(2):
            pair_t = q_ref[0, qs, r * LANES:(r + 1) * LANES].astype(F32).T
            zero = jnp.zeros_like(pair_t)
            qt_sc[:, (2 * r) * tq:(2 * r + 1) * tq] = jnp.where(feat_lo, pair_t, zero).astype(BF16)
            qt_sc[:, (2 * r + 1) * tq:(2 * r + 2) * tq] = jnp.where(feat_lo, zero, pair_t).astype(BF16)
        m_sc[...] = jnp.full_like(m_sc, -jnp.inf)
        acc_sc[...] = jnp.zeros_like(acc_sc)

        key_i = lax.broadcasted_iota(jnp.int32, (tq, 4 * tq), 0)
        qt_i = lax.broadcasted_iota(jnp.int32, (tq, 4 * tq), 1) & (tq - 1)
        scores(qi * tq, stage_a, causal=key_i <= qt_i)

        def kv_pair(t, start_in_a):
            scores(2 * t * tq, stage_b)
            consume(start_in_a, stage_a)
            scores((2 * t + 1) * tq, stage_a)
            consume(2 * t * tq, stage_b)
            return (2 * t + 1) * tq

        start_in_a = lax.fori_loop(0, qi // 2, kv_pair, qi * tq)

        @pl.when(qi % 2 == 1)
        def _():
            scores((qi - 1) * tq, stage_b)
            consume(start_in_a, stage_a)
            consume((qi - 1) * tq, stage_b)

        @pl.when(qi % 2 == 0)
        def _():
            consume(start_in_a, stage_a)

        on = acc_sc[0:HEAD_W, :] / acc_sc[HEAD_W:HEAD_W + 1, :]
        for r in range(2):
            o_t = on[:, (2 * r) * tq:(2 * r + 1) * tq] - lam * on[:, (2 * r + 1) * tq:(2 * r + 2) * tq]
            o = o_t.T
            ms = jnp.mean(o * o, axis=-1, keepdims=True)
            o = o * lax.rsqrt(ms + SUBLN_EPS) * sw * (1.0 - LAMBDA_INIT)
            o_ref[0, qs, r * LANES:(r + 1) * LANES] = o.astype(o_ref.dtype)
        return carry

    lax.fori_loop(0, seq // tq, q_block, 0)


def _attn_prompt(q, k, v, lam_vecs, subln_w, tq=256):
    b, seq, _ = q.shape
    kv_spec = pl.BlockSpec((None, seq * N_KV_HEADS, HEAD_W), lambda i, h: (i, 0, 0))
    return pl.pallas_call(
        functools.partial(_attn_prompt_kernel, tq=tq, seq=seq),
        grid=(b, N_KV_HEADS),
        in_specs=[_resident(lam_vecs.shape),
                  pl.BlockSpec((1, seq, 2 * HEAD_W), lambda i, h: (i, 0, h)),
                  kv_spec, kv_spec, _resident((1, HEAD_W))],
        out_specs=pl.BlockSpec((1, seq, 2 * HEAD_W), lambda i, h: (i, 0, h)),
        out_shape=jax.ShapeDtypeStruct((b, seq, ATT_WIDTH), BF16),
        scratch_shapes=[pltpu.VMEM((N_KV_HEADS, seq, HEAD_W), BF16),
                        pltpu.VMEM((N_KV_HEADS, HEAD_W + bf16_tile_rows, seq), BF16),
                        pltpu.VMEM((HEAD_W, 4 * tq), BF16), pltpu.VMEM((1, 4 * tq), F32),
                        pltpu.VMEM((HEAD_W + bf16_tile_rows, 4 * tq), F32),
                        pltpu.VMEM((tq, 4 * tq), F32), pltpu.VMEM((1, 4 * tq), F32),
                        pltpu.VMEM((tq, 4 * tq), F32), pltpu.VMEM((1, 4 * tq), F32)],
        compiler_params=pltpu.CompilerParams(dimension_semantics=("parallel", "arbitrary"),
                                             vmem_limit_bytes=VMEM_LIMIT),
        name="attn_prompt",
    )(lam_vecs, q, k, v, subln_w)


def _col_broadcast(row):
    return jnp.broadcast_to(row, (LANES, LANES)).T


def _attn_paged_kernel(pt_ref, lam_ref, q_ref, kn_ref, vn_ref, sw_ref, ck_hbm, cv_hbm, o_ref, kbuf, vbuf, sem,
                       qall_sc, sa_sc, smaxa_sc, sb_sc, smaxb_sc, m_sc, l_sc, acc_sc, *,
                       pages_per_group, n_groups, n_batch, t_new):
    b = pl.program_id(0)
    n_slots = kbuf.shape[0]
    lookahead = n_slots - 2
    n_pages = pages_per_group * n_groups
    rows_pg = PAGE_SIZE * N_KV_HEADS
    head_shift = int(math.log2(N_KV_HEADS))
    col_shift = int(math.log2(4 * t_new))
    row_i = lax.broadcasted_iota(jnp.int32, (rows_pg, LANES), 0)
    col_i = lax.broadcasted_iota(jnp.int32, (rows_pg, LANES), 1)
    head_match = (row_i & (N_KV_HEADS - 1)) == (col_i >> col_shift)

    def group_copies(batch, g):
        slot = g % n_slots
        copies = []
        for i in range(pages_per_group):
            page = pt_ref[batch * n_pages + g * pages_per_group + i]
            dst = pl.ds(i * rows_pg, rows_pg)
            copies.append(pltpu.make_async_copy(ck_hbm.at[page], kbuf.at[slot, dst], sem.at[0, slot]))
            copies.append(pltpu.make_async_copy(cv_hbm.at[page], vbuf.at[slot, dst], sem.at[1, slot]))
        return copies

    def start_group(batch, g):
        for cp in group_copies(batch, g):
            cp.start()

    def wait_group(batch, g):
        for cp in group_copies(batch, g):
            cp.wait()

    @pl.when(b == 0)
    def _():
        for g in range(lookahead):
            start_group(b, g)

    lane_lo = lax.broadcasted_iota(jnp.int32, (t_new, LANES), 1) < ATT_HEAD_DIM
    pieces = []
    for hr in range(2 * N_KV_HEADS):
        pair = q_ref[0, :, hr * HEAD_W:(hr + 1) * HEAD_W].astype(F32)
        zero = jnp.zeros_like(pair)
        pieces += [jnp.where(lane_lo, pair, zero), jnp.where(lane_lo, zero, pair)]
    qall_sc[...] = jnp.concatenate(pieces, axis=0).T.astype(BF16)
    m_sc[...] = jnp.full_like(m_sc, -jnp.inf)
    l_sc[...] = jnp.zeros_like(l_sc)
    acc_sc[...] = jnp.zeros_like(acc_sc)

    def stage_scores(stage, n_blocks, rows, score_fn):
        s_ref, smax_ref = stage
        smax = jnp.full((1, LANES), -jnp.inf, F32)
        for i in range(n_blocks):
            s = score_fn(i)
            s_ref[i * rows_pg:i * rows_pg + rows, :] = s
            smax = jnp.maximum(smax, jnp.max(s, axis=0, keepdims=True))
        smax_ref[...] = smax

    def consume(stage, n_blocks, rows, value_fn):
        s_ref, smax_ref = stage
        m_prev = m_sc[...]
        m_new = jnp.maximum(m_prev, smax_ref[...])
        alpha = jnp.exp2(m_prev - m_new)
        lsum = jnp.zeros_like(m_new)
        pv = jnp.zeros((LANES, HEAD_W), F32)
        for i in range(n_blocks):
            p = jnp.exp2(s_ref[i * rows_pg:i * rows_pg + rows, :] - m_new)
            lsum = lsum + jnp.sum(p, axis=0, keepdims=True)
            pv = pv + jnp.dot(p.T.astype(BF16), value_fn(i), preferred_element_type=F32)
        l_sc[...] = alpha * l_sc[...] + lsum
        acc_sc[...] = _col_broadcast(alpha) * acc_sc[...] + pv
        m_sc[...] = m_new

    stages = ((sa_sc, smaxa_sc), (sb_sc, smaxb_sc))

    def group_scores(g):
        slot = g % n_slots

        def page_scores(i):
            page = kbuf[slot, i * rows_pg:(i + 1) * rows_pg, :].astype(BF16)
            return jnp.where(head_match, jnp.dot(page, qall_sc[...], preferred_element_type=F32), -jnp.inf)

        stage_scores(stages[g % 2], pages_per_group, rows_pg, page_scores)

    def group_consume(g):
        slot = g % n_slots
        consume(stages[g % 2], pages_per_group, rows_pg,
                lambda i: vbuf[slot, i * rows_pg:(i + 1) * rows_pg, :].astype(BF16))

    for g in range(n_groups):
        wait_group(b, g)
        nxt = g + lookahead
        if nxt < n_groups:
            start_group(b, nxt)
        else:
            @pl.when(b + 1 < n_batch)
            def _():
                start_group(b + 1, nxt - n_groups)
        group_scores(g)
        if g >= 1:
            group_consume(g - 1)
    group_consume(n_groups - 1)

    pad = jnp.zeros((LANES - t_new * N_KV_HEADS, HEAD_W), F32)
    kn = jnp.concatenate([kn_ref[0], pad], axis=0).astype(BF16)
    vn = jnp.concatenate([vn_ref[0], pad], axis=0).astype(BF16)
    row_n = lax.broadcasted_iota(jnp.int32, (LANES, LANES), 0)
    col_n = lax.broadcasted_iota(jnp.int32, (LANES, LANES), 1)
    live = (((row_n & (N_KV_HEADS - 1)) == (col_n >> col_shift))
            & ((row_n >> head_shift) <= (col_n & (t_new - 1))))
    s_new = jnp.where(live, jnp.dot(kn, qall_sc[...], preferred_element_type=F32), -jnp.inf)
    stage_scores(stages[0], 1, LANES, lambda i: s_new)
    consume(stages[0], 1, LANES, lambda i: vn)
    lam = _lambda(lam_ref)
    sw = sw_ref[...]
    on = acc_sc[...] / _col_broadcast(l_sc[...])
    for hr in range(2 * N_KV_HEADS):
        base = hr * 2 * t_new
        o = on[base:base + t_new, :] - lam * on[base + t_new:base + 2 * t_new, :]
        ms = jnp.mean(o * o, axis=-1, keepdims=True)
        o = o * lax.rsqrt(ms + SUBLN_EPS) * sw * (1.0 - LAMBDA_INIT)
        o_ref[0, :, hr * HEAD_W:(hr + 1) * HEAD_W] = o.astype(o_ref.dtype)


def _attn_paged(q, k_new, v_new, cache_k, cache_v, page_table, lam_vecs, subln_w, pages_per_group=4, n_slots=4):
    b, t_new, _ = q.shape
    n_pages = page_table.shape[1]
    n_groups = n_pages // pages_per_group
    rows_pg = PAGE_SIZE * N_KV_HEADS
    assert N_KV_HEADS * 4 * t_new == LANES
    assert n_groups % n_slots == 0 and n_groups >= n_slots
    pt_flat = page_table.reshape(-1)

    per_b = lambda r, w: pl.BlockSpec((1, r, w), lambda bi, pt: (bi, 0, 0))
    const = lambda shape: pl.BlockSpec(shape, lambda bi, pt: (0,) * len(shape))
    hbm = pl.BlockSpec(memory_space=pl.ANY)
    group_rows = pages_per_group * rows_pg
    grid_spec = pltpu.PrefetchScalarGridSpec(
        num_scalar_prefetch=1,
        grid=(b,),
        in_specs=[const(lam_vecs.shape), per_b(t_new, ATT_WIDTH), per_b(t_new * N_KV_HEADS, HEAD_W),
                  per_b(t_new * N_KV_HEADS, HEAD_W), const((1, HEAD_W)), hbm, hbm],
        out_specs=per_b(t_new, ATT_WIDTH),
        scratch_shapes=[pltpu.VMEM((n_slots, group_rows, HEAD_W), F32), pltpu.VMEM((n_slots, group_rows, HEAD_W), F32),
                        pltpu.SemaphoreType.DMA((2, n_slots)),
                        pltpu.VMEM((HEAD_W, LANES), BF16),
                        pltpu.VMEM((group_rows, LANES), F32), pltpu.VMEM((1, LANES), F32),
                        pltpu.VMEM((group_rows, LANES), F32), pltpu.VMEM((1, LANES), F32),
                        pltpu.VMEM((1, LANES), F32), pltpu.VMEM((1, LANES), F32), pltpu.VMEM((LANES, HEAD_W), F32)],
    )
    return pl.pallas_call(
        functools.partial(_attn_paged_kernel, pages_per_group=pages_per_group, n_groups=n_groups, n_batch=b,
                          t_new=t_new),
        grid_spec=grid_spec,
        out_shape=jax.ShapeDtypeStruct((b, t_new, ATT_WIDTH), BF16),
        compiler_params=pltpu.CompilerParams(dimension_semantics=("arbitrary",),
                                             vmem_limit_bytes=VMEM_LIMIT),
        name="attn_paged",
    )(pt_flat, lam_vecs, q, k_new, v_new, subln_w, cache_k, cache_v)


def _mix_kernel(y_ref, z_ref, o_ref, g_ref, x_ref, nw_ref, wso_ref, wao_ref, wo_ref, out_ref):
    y = y_ref[...].astype(F32) * _silu(z_ref[...].astype(F32))
    gw = D_INNER // N_SSM_GROUPS
    parts = []
    for g in range(N_SSM_GROUPS):
        yg = y[:, g * gw:(g + 1) * gw]
        ms = jnp.mean(yg * yg, axis=-1, keepdims=True)
        parts.append((yg * lax.rsqrt(ms + NORM_EPS) * nw_ref[:, g * gw:(g + 1) * gw]).astype(BF16))
    y_ssm = jnp.dot(jnp.concatenate(parts, axis=1), wso_ref[...], preferred_element_type=F32)
    y_att = jnp.dot(o_ref[...], wao_ref[...], preferred_element_type=F32)
    gates = jax.nn.sigmoid(g_ref[...].astype(F32))
    mixed = (gates[:, :D_MODEL] * y_ssm + gates[:, D_MODEL:] * y_att).astype(BF16)
    out_ref[...] = x_ref[...] + jnp.dot(mixed, wo_ref[...], preferred_element_type=F32)


def _mix(y, z, o, gates, x, ssm_norm_w, w_ssm_out, w_att_out, w_o, tm=512):
    n = x.shape[0]
    row = lambda w: pl.BlockSpec((tm, w), lambda i: (i, 0))
    return pl.pallas_call(
        _mix_kernel,
        grid=(n // tm,),
        in_specs=[row(D_INNER), row(D_INNER), row(ATT_WIDTH), row(2 * D_MODEL), row(D_MODEL),
                  _resident((1, D_INNER)), _resident((D_INNER, D_MODEL)), _resident((ATT_WIDTH, D_MODEL)),
                  _resident((D_MODEL, D_MODEL))],
        out_specs=row(D_MODEL),
        out_shape=jax.ShapeDtypeStruct((n, D_MODEL), F32),
        compiler_params=pltpu.CompilerParams(dimension_semantics=("parallel",), vmem_limit_bytes=VMEM_LIMIT),
        name="mix",
    )(y, z, o, gates, x, ssm_norm_w, w_ssm_out, w_att_out, w_o)


def _ffn_kernel(x_ref, nw_ref, win_ref, wout_ref, fw_ref, out_ref, *, chunk):
    x = x_ref[...]
    ms = jnp.mean(x * x, axis=-1, keepdims=True)
    h = (x * lax.rsqrt(ms + NORM_EPS) * nw_ref[...]).astype(BF16)
    acc = x
    for c in range(0, D_FF, chunk):
        gate = jnp.dot(h, win_ref[:, c:c + chunk], preferred_element_type=F32)
        up = jnp.dot(h, win_ref[:, D_FF + c:D_FF + c + chunk], preferred_element_type=F32)
        hid = (_silu(gate) * up).astype(BF16)
        acc = acc + jnp.dot(hid, wout_ref[c:c + chunk, :], preferred_element_type=F32)
    ms2 = jnp.mean(acc * acc, axis=-1, keepdims=True)
    out_ref[...] = acc * lax.rsqrt(ms2 + NORM_EPS) * fw_ref[...]


def _ffn(x, norm_w, w_in, w_out, final_w, tm=512, chunk=256):
    n = x.shape[0]
    row = pl.BlockSpec((tm, D_MODEL), lambda i: (i, 0))
    return pl.pallas_call(
        functools.partial(_ffn_kernel, chunk=chunk),
        grid=(n // tm,),
        in_specs=[row, _resident((1, D_MODEL)), _resident((D_MODEL, 2 * D_FF)), _resident((D_FF, D_MODEL)),
                  _resident((1, D_MODEL))],
        out_specs=row,
        out_shape=jax.ShapeDtypeStruct((n, D_MODEL), F32),
        compiler_params=pltpu.CompilerParams(dimension_semantics=("parallel",), vmem_limit_bytes=VMEM_LIMIT),
        name="ffn",
    )(x, norm_w, w_in, w_out, final_w)


def _pack_w_in(w_in):
    cuts = np.cumsum((D_INNER, CONV_DIM, N_SSM_HEADS, ATT_WIDTH, KV_WIDTH, KV_WIDTH, 2 * D_MODEL))[:-1]
    wz, wxbc, wdt, wq, wk, wv, wg = jnp.split(w_in, [int(c) for c in cuts], axis=-1)
    wdt = jnp.pad(wdt, ((0, 0), (0, DT_PAD - N_SSM_HEADS)))
    wq = wq * (ATT_HEAD_DIM ** -0.5 * math.log2(math.e))
    return jnp.concatenate([wz, wxbc, wq, wk, wv, wg, wdt], axis=-1).astype(BF16)


def kernel(x_prompt, x_sample, cache_k, cache_v, state_conv, state_ssm, page_table, norm_mix_w, w_in, conv_w, conv_b, dt_bias, a_log, d_skip, ssm_norm_w, w_ssm_out, lambda_q1, lambda_k1, lambda_q2, lambda_k2, subln_w, w_att_out, w_o, norm_ffn_w, w_ffn_in, w_ffn_out, norm_final_w):
    l = 0
    w_all = _pack_w_in(w_in[l])
    pad_heads = lambda t: jnp.pad(t.reshape(1, N_SSM_HEADS), ((0, 0), (0, DT_PAD - N_SSM_HEADS)))
    lam_vecs = jnp.stack([lambda_q1[l], lambda_k1[l], lambda_q2[l], lambda_k2[l]])
    params = dict(
        gain=norm_mix_w[l].reshape(1, D_MODEL), conv_w=conv_w[l], conv_b=conv_b[l].reshape(1, CONV_DIM),
        dt_bias=pad_heads(dt_bias[l]), a_log=pad_heads(a_log[l]),
        d_skip_x=jnp.repeat(d_skip[l], SSM_HEAD_DIM).reshape(1, D_INNER),
        ssm_norm_w=ssm_norm_w[l].reshape(1, D_INNER), w_ssm_out=w_ssm_out[l].astype(BF16),
        subln_w=subln_w[l].reshape(1, 2 * ATT_HEAD_DIM), w_att_out=w_att_out[l].astype(BF16),
        w_o=w_o[l].astype(BF16), norm_ffn_w=norm_ffn_w[l].reshape(1, D_MODEL),
        w_ffn_in=w_ffn_in[l].astype(BF16), w_ffn_out=w_ffn_out[l].astype(BF16),
        norm_final_w=norm_final_w.reshape(1, D_MODEL))

    def group(x, conv0, h0, attend):
        b, seq, _ = x.shape
        n = b * seq
        x2 = x.reshape(n, D_MODEL)
        z, xbc, q, k, v, gates, dt = _inproj(x2, params["gain"], w_all)
        y, conv_state, ssm_state = _ssd(
            xbc.reshape(b, seq, CONV_DIM), dt.reshape(b, seq, DT_PAD), conv0, h0.reshape(b, D_INNER, D_STATE),
            params["conv_w"], params["conv_b"], params["dt_bias"], params["a_log"], params["d_skip_x"])
        k3 = k.reshape(b, seq * N_KV_HEADS, HEAD_W)
        v3 = v.reshape(b, seq * N_KV_HEADS, HEAD_W)
        o = attend(q.reshape(b, seq, ATT_WIDTH), k3, v3)
        x1 = _mix(y.reshape(n, D_INNER), z, o.reshape(n, ATT_WIDTH), gates, x2, params["ssm_norm_w"],
                  params["w_ssm_out"], params["w_att_out"], params["w_o"])
        out = _ffn(x1, params["norm_ffn_w"], params["w_ffn_in"], params["w_ffn_out"], params["norm_final_w"])
        kv_shape = (1, b, seq, N_KV_HEADS, 2 * ATT_HEAD_DIM)
        return (out.reshape(b, seq, D_MODEL), k.reshape(kv_shape), v.reshape(kv_shape), conv_state[None],
                ssm_state.reshape(1, b, N_SSM_HEADS, SSM_HEAD_DIM, D_STATE))

    bp = x_prompt.shape[0]
    attend_p = lambda q, k, v: _attn_prompt(q, k, v, lam_vecs, params["subln_w"])
    yp, kp, vp, cp, hp = group(x_prompt, jnp.zeros((bp, D_CONV - 1, CONV_DIM), F32),
                               jnp.zeros((bp, N_SSM_HEADS, SSM_HEAD_DIM, D_STATE), F32), attend_p)
    n_pool = cache_k.shape[1]
    ck = cache_k[l].reshape(n_pool, PAGE_SIZE * N_KV_HEADS, HEAD_W)
    cv = cache_v[l].reshape(n_pool, PAGE_SIZE * N_KV_HEADS, HEAD_W)
    attend_s = lambda q, k, v: _attn_paged(q, k, v, ck, cv, page_table, lam_vecs, params["subln_w"])
    ys, ks, vs, cs, hs = group(x_sample, state_conv[l], state_ssm[l], attend_s)
    return (yp, ys, kp, vp, cp, hp, ks, vs, cs, hs)
```

```python
import functools
import math

import jax
import jax.numpy as jnp
import numpy as np
from jax import lax
from jax.experimental import pallas as pl
from jax.experimental.pallas import tpu as pltpu

F32 = jnp.float32
BF16 = jnp.bfloat16

D_MODEL = 1024
D_INNER = 2048
SSM_HEAD_DIM = 64
N_SSM_HEADS = 32
N_SSM_GROUPS = 4
D_STATE = 128
D_CONV = 4
CONV_DIM = 3072
ATT_HEAD_DIM = 64
N_KV_HEADS = 4
ATT_WIDTH = 1024
KV_WIDTH = 512
D_FF = 2816
PAGE_SIZE = 128
HEAD_W = 2 * ATT_HEAD_DIM
NORM_EPS = 1e-6
SUBLN_EPS = 1e-5
LAMBDA_INIT = 0.8 - 0.6 * math.exp(-0.3 * 0)

LANES = 128
bf16_tile_rows = 16
CHUNK = 128
DT_PAD = LANES
VMEM_LIMIT = 56 * 1024 * 1024

SEG_Z, SEG_XBC, SEG_Q, SEG_K, SEG_V, SEG_G, SEG_DT = 0, 2048, 5120, 6144, 6656, 7168, 9216
W_ALL = SEG_DT + DT_PAD


def _resident(shape):
    nd = len(shape)
    return pl.BlockSpec(shape, lambda *_: (0,) * nd, pipeline_mode=pl.Buffered(1))


def _silu_tanh(x):
    h = 0.5 * x
    return h + h * jnp.tanh(h)


def _inproj_kernel(x_ref, g_ref, w_ref, *out_refs, segs):
    x = x_ref[...]
    tm = x_ref.shape[0]
    ms = jnp.mean(x * x, axis=-1, keepdims=True)
    h = (x * lax.rsqrt(ms + NORM_EPS) * g_ref[...]).astype(BF16)
    for ref, (start, width, _, kv_rows) in zip(out_refs, segs):
        step = min(width, 512)
        for c in range(0, width, step):
            r = jnp.dot(h, w_ref[:, start + c:start + c + step], preferred_element_type=F32)
            if kv_rows:
                for hd in range(c // HEAD_W, (c + step) // HEAD_W):
                    ref[pl.ds(hd, tm, stride=N_KV_HEADS), :] = r[:, hd * HEAD_W - c:(hd + 1) * HEAD_W - c]
            else:
                ref[:, c:c + step] = r.astype(ref.dtype)


def _inproj_call(x2d, gain, w, segs, tm):
    n = x2d.shape[0]
    specs, shapes = [], []
    for _, width, dtype, kv_rows in segs:
        if kv_rows:
            specs.append(pl.BlockSpec((tm * N_KV_HEADS, HEAD_W), lambda i: (i, 0)))
            shapes.append(jax.ShapeDtypeStruct((n * N_KV_HEADS, HEAD_W), dtype))
        else:
            specs.append(pl.BlockSpec((tm, width), lambda i: (i, 0)))
            shapes.append(jax.ShapeDtypeStruct((n, width), dtype))
    return pl.pallas_call(
        functools.partial(_inproj_kernel, segs=segs),
        grid=(n // tm,),
        in_specs=[pl.BlockSpec((tm, D_MODEL), lambda i: (i, 0)), _resident((1, D_MODEL)), _resident(w.shape)],
        out_specs=specs,
        out_shape=shapes,
        compiler_params=pltpu.CompilerParams(dimension_semantics=("parallel",), vmem_limit_bytes=VMEM_LIMIT),
        name="in_proj",
    )(x2d, gain, w)


def _inproj(x2d, gain, w_ssm, w_rest, tm=1024):
    tm = min(tm, x2d.shape[0])
    z, xbc = _inproj_call(x2d, gain, w_ssm, ((SEG_Z, D_INNER, BF16, False), (SEG_XBC, CONV_DIM, BF16, False)), tm)
    off = SEG_Q
    q, k, v, gates, dt = _inproj_call(
        x2d, gain, w_rest,
        ((SEG_Q - off, ATT_WIDTH, BF16, False), (SEG_K - off, KV_WIDTH, F32, True), (SEG_V - off, KV_WIDTH, F32, True),
         (SEG_G - off, 2 * D_MODEL, BF16, False), (SEG_DT - off, DT_PAD, F32, False)), tm)
    return z, xbc, q, k, v, gates, dt


def _split3_packed(v):
    a = v.astype(BF16).astype(F32)
    r = v - a
    b = r.astype(BF16).astype(F32)
    c = (r - b).astype(BF16).astype(F32)
    packed = a + pltpu.roll(b, N_SSM_HEADS, axis=1) + pltpu.roll(c, 2 * N_SSM_HEADS, axis=1)
    return packed.astype(BF16)


def _ssd_kernel(xbc_ref, dt_ref, pre_ref, st0_ref, cw_ref, cb_ref, dtb_ref, alog_ref, dsk_ref, e64_ref, e128_ref,
                y_ref, conv_ref, st_ref, xp_sc, act_sc, st_sc, cst_sc, *, t_in, n_chunks):
    c = pl.program_id(1)
    T = CHUNK
    row_pad = 8

    @pl.when(c == 0)
    def _():
        xp_sc[...] = jnp.zeros_like(xp_sc)
        xp_sc[row_pad - (D_CONV - 1):row_pad, :] = pre_ref[0]
        for blk in range(D_INNER // LANES):
            st_sc[:, blk * LANES:(blk + 1) * LANES] = st0_ref[0, blk * LANES:(blk + 1) * LANES, :].T

    xp_sc[row_pad:row_pad + t_in, :] = xbc_ref[0].astype(F32)

    cblk = 256
    for j in range(CONV_DIM // cblk):
        cs_ = slice(j * cblk, (j + 1) * cblk)
        window = xp_sc[:, cs_]
        acc = cb_ref[:, cs_] + window[row_pad:] * cw_ref[D_CONV - 1:D_CONV, cs_]
        for k in range(D_CONV - 1):
            shifted = pltpu.roll(window, D_CONV - 1 - k, axis=0)[row_pad:]
            acc = acc + shifted * cw_ref[k:k + 1, cs_]
        act_sc[:, cs_] = _silu_tanh(acc)

    lane = lax.broadcasted_iota(jnp.int32, (T, LANES), 1)
    rowi = lax.broadcasted_iota(jnp.int32, (T, LANES), 0)
    if t_in < T:
        dt_raw = jnp.concatenate([dt_ref[0], jnp.zeros((T - t_in, DT_PAD), F32)], axis=0)
    else:
        dt_raw = dt_ref[0]
    v = dt_raw + dtb_ref[...]
    softplus = jnp.maximum(v, 0.0) + jnp.log1p(jnp.exp(-jnp.abs(v)))
    dt = jnp.where((lane < N_SSM_HEADS) & (rowi < t_in), softplus, 0.0)
    la = dt * (-jnp.exp(alog_ref[...]))
    cs = la
    d = 1
    while d < T:
        cs = cs + jnp.where(rowi >= d, pltpu.roll(cs, d, axis=0), 0.0)
        d *= 2
    cs = cs * math.log2(math.e)
    cst_sc[...] = cs.T
    x_dt = _split3_packed(dt)
    x_cs = _split3_packed(cs)
    causal = rowi >= lane
    lo_half = lane < SSM_HEAD_DIM

    heads_per_group = N_SSM_HEADS // N_SSM_GROUPS
    gw = heads_per_group * SSM_HEAD_DIM
    for g in range(N_SSM_GROUPS):
        bm = act_sc[:, D_INNER + g * D_STATE:D_INNER + (g + 1) * D_STATE]
        cm = act_sc[:, D_INNER + N_SSM_GROUPS * D_STATE + g * D_STATE:
                    D_INNER + N_SSM_GROUPS * D_STATE + (g + 1) * D_STATE]
        bm_b = bm.astype(BF16)
        cm_b = cm.astype(BF16)
        bmt_b = bm.T.astype(BF16)
        cb = lax.dot_general(cm_b, bm_b, (((1,), (1,)), ((), ())), preferred_element_type=F32)
        gs = slice(g * gw, (g + 1) * gw)
        dtx = jnp.dot(x_dt, e64_ref[:, gs], preferred_element_type=F32)
        csx = jnp.dot(x_cs, e64_ref[:, gs], preferred_element_type=F32)
        cs_last = csx[T - 1:T, :]
        xs = act_sc[:, gs]
        xd = xs * dtx
        xdd_b = (xd * jnp.exp2(cs_last - csx)).astype(BF16)
        xd_b = xd.astype(BF16)
        st_old = st_sc[:, gs]
        y_off = jnp.dot(cm_b, st_old.astype(BF16), preferred_element_type=F32) * jnp.exp2(csx)
        st_sc[:, gs] = st_old * jnp.exp2(cs_last) + jnp.dot(bmt_b, xdd_b, preferred_element_type=F32)
        y_skip = xs * dsk_ref[:, gs]
        for pr in range(heads_per_group // 2):
            ms = []
            for hh in range(2):
                h = g * heads_per_group + pr * 2 + hh
                cs_col = jnp.dot(x_cs, e128_ref[:, h * LANES:(h + 1) * LANES], preferred_element_type=F32)
                seg = jnp.where(causal, cs_col - cst_sc[h:h + 1, :], -jnp.inf)
                ms.append((cb * jnp.exp2(seg)).astype(BF16))
            ps = slice(pr * LANES, (pr + 1) * LANES)
            xd_p = xd_b[:, ps]
            zero = jnp.zeros_like(xd_p)
            rhs = jnp.concatenate([jnp.where(lo_half, xd_p, zero), jnp.where(lo_half, zero, xd_p)], axis=0)
            y_diag = jnp.dot(jnp.concatenate(ms, axis=1), rhs, preferred_element_type=F32)
            y = y_diag + y_off[:, ps] + y_skip[:, ps]
            y_ref[0, :, g * gw + pr * LANES:g * gw + (pr + 1) * LANES] = y[:t_in].astype(y_ref.dtype)

    @pl.when(c == n_chunks - 1)
    def _():
        conv_ref[0] = xp_sc[row_pad + t_in - (D_CONV - 1):row_pad + t_in, :]
        for blk in range(D_INNER // LANES):
            st_ref[0, blk * LANES:(blk + 1) * LANES, :] = st_sc[:, blk * LANES:(blk + 1) * LANES].T

    if n_chunks > 1:
        xp_sc[0:row_pad, :] = xp_sc[T:T + row_pad, :]


def _expansion_matrices():
    e64 = np.zeros((LANES, D_INNER), np.float32)
    e128 = np.zeros((LANES, N_SSM_HEADS * LANES), np.float32)
    for part in range(3):
        for h in range(N_SSM_HEADS):
            e64[part * N_SSM_HEADS + h, h * SSM_HEAD_DIM:(h + 1) * SSM_HEAD_DIM] = 1.0
            e128[part * N_SSM_HEADS + h, h * LANES:(h + 1) * LANES] = 1.0
    return jnp.asarray(e64, BF16), jnp.asarray(e128, BF16)


def _ssd(xbc, dt, prefix, st0, conv_w, conv_b, dt_bias, a_log, d_skip_x):
    b, l, _ = xbc.shape
    t_in = min(l, CHUNK)
    n_chunks = l // t_in
    e64, e128 = _expansion_matrices()
    per_b = lambda shape: pl.BlockSpec((1,) + shape, lambda i, c: (i, 0, 0))
    per_chunk = lambda w: pl.BlockSpec((1, t_in, w), lambda i, c: (i, c, 0))
    return pl.pallas_call(
        functools.partial(_ssd_kernel, t_in=t_in, n_chunks=n_chunks),
        grid=(b, n_chunks),
        in_specs=[per_chunk(CONV_DIM), per_chunk(DT_PAD), per_b((D_CONV - 1, CONV_DIM)), per_b((D_INNER, D_STATE)),
                  _resident((D_CONV, CONV_DIM)), _resident((1, CONV_DIM)), _resident((1, DT_PAD)),
                  _resident((1, DT_PAD)), _resident((1, D_INNER)), _resident(e64.shape), _resident(e128.shape)],
        out_specs=[per_chunk(D_INNER), per_b((D_CONV - 1, CONV_DIM)), per_b((D_INNER, D_STATE))],
        out_shape=[jax.ShapeDtypeStruct((b, l, D_INNER), BF16),
                   jax.ShapeDtypeStruct((b, D_CONV - 1, CONV_DIM), F32),
                   jax.ShapeDtypeStruct((b, D_INNER, D_STATE), F32)],
        scratch_shapes=[pltpu.VMEM((8 + CHUNK, CONV_DIM), F32), pltpu.VMEM((CHUNK, CONV_DIM), F32),
                        pltpu.VMEM((D_STATE, D_INNER), F32), pltpu.VMEM((LANES, CHUNK), F32)],
        compiler_params=pltpu.CompilerParams(dimension_semantics=("parallel", "arbitrary"),
                                             vmem_limit_bytes=VMEM_LIMIT),
        name="ssd",
    )(xbc, dt, prefix, st0, conv_w, conv_b, dt_bias, a_log, d_skip_x, e64, e128)


def _lambda(lam_ref):
    lv = lam_ref[...]
    s1 = jnp.sum(lv[0:1] * lv[1:2], axis=-1, keepdims=True)
    s2 = jnp.sum(lv[2:3] * lv[3:4], axis=-1, keepdims=True)
    return jnp.exp(s1) - jnp.exp(s2) + LAMBDA_INIT


def _attn_prompt_kernel(lam_ref, q_ref, k_ref, v_ref, sw_ref, o_ref, kall_sc, vtall_sc, qt_sc, m_sc, acc_sc,
                        sa_sc, smaxa_sc, sb_sc, smaxb_sc, *, tq, seq):
    h = pl.program_id(1)
    extra = vtall_sc.shape[1] - HEAD_W

    @pl.when(h == 0)
    def _():
        ones_row = (lax.broadcasted_iota(jnp.int32, (extra, seq), 0) == 0).astype(F32).astype(BF16)
        for hd in range(N_KV_HEADS):
            kall_sc[hd] = k_ref[pl.ds(hd, seq, stride=N_KV_HEADS), :].astype(BF16)
            for blk in range(seq // LANES):
                vb = v_ref[pl.ds(hd + blk * LANES * N_KV_HEADS, LANES, stride=N_KV_HEADS), :]
                vtall_sc[hd, 0:HEAD_W, blk * LANES:(blk + 1) * LANES] = vb.T.astype(BF16)
            vtall_sc[hd, HEAD_W:, :] = ones_row

    k_sc = kall_sc.at[h]
    vt_sc = vtall_sc.at[h]
    lam = _lambda(lam_ref)
    sw = sw_ref[...]
    feat_lo = lax.broadcasted_iota(jnp.int32, (LANES, tq), 0) < ATT_HEAD_DIM

    def kv_slice(start):
        return pl.ds(pl.multiple_of(start, tq), tq)

    def scores(start, stage, causal=None):
        s_ref, smax_ref = stage
        s = jnp.dot(k_sc[kv_slice(start), :], qt_sc[...], preferred_element_type=F32)
        if causal is not None:
            s = jnp.where(causal, s, -jnp.inf)
        s_ref[...] = s
        smax_ref[...] = jnp.max(s, axis=0, keepdims=True)

    def consume(start, stage):
        s_ref, smax_ref = stage
        m_prev = m_sc[...]
        m_new = jnp.maximum(m_prev, smax_ref[...])
        alpha = jnp.exp2(m_prev - m_new)
        p = jnp.exp2(s_ref[...] - m_new)
        pv = jnp.dot(vt_sc[:, kv_slice(start)], p.astype(BF16), preferred_element_type=F32)
        acc_sc[...] = alpha * acc_sc[...] + pv
        m_sc[...] = m_new

    stage_a, stage_b = (sa_sc, smaxa_sc), (sb_sc, smaxb_sc)

    def q_block(qi, carry):
        qs = pl.ds(pl.multiple_of(qi * tq, tq), tq)
        for r in range(2):
            pair_t = q_ref[0, qs, r * LANES:(r + 1) * LANES].astype(F32).T
            zero = jnp.zeros_like(pair_t)
            qt_sc[:, (2 * r) * tq:(2 * r + 1) * tq] = jnp.where(feat_lo, pair_t, zero).astype(BF16)
            qt_sc[:, (2 * r + 1) * tq:(2 * r + 2) * tq] = jnp.where(feat_lo, zero, pair_t).astype(BF16)
        m_sc[...] = jnp.full_like(m_sc, -jnp.inf)
        acc_sc[...] = jnp.zeros_like(acc_sc)

        key_i = lax.broadcasted_iota(jnp.int32, (tq, 4 * tq), 0)
        qt_i = lax.broadcasted_iota(jnp.int32, (tq, 4 * tq), 1) & (tq - 1)
        scores(qi * tq, stage_a, causal=key_i <= qt_i)

        def kv_pair(t, start_in_a):
            scores(2 * t * tq, stage_b)
            consume(start_in_a, stage_a)
            scores((2 * t + 1) * tq, stage_a)
            consume(2 * t * tq, stage_b)
            return (2 * t + 1) * tq

        start_in_a = lax.fori_loop(0, qi // 2, kv_pair, qi * tq)

        @pl.when(qi % 2 == 1)
        def _():
            scores((qi - 1) * tq, stage_b)
            consume(start_in_a, stage_a)
            consume((qi - 1) * tq, stage_b)

        @pl.when(qi % 2 == 0)
        def _():
            consume(start_in_a, stage_a)

        on = acc_sc[0:HEAD_W, :] / acc_sc[HEAD_W:HEAD_W + 1, :]
        for r in range(2):
            o_t = on[:, (2 * r) * tq:(2 * r + 1) * tq] - lam * on[:, (2 * r + 1) * tq:(2 * r + 2) * tq]
            o = o_t.T
            ms = jnp.mean(o * o, axis=-1, keepdims=True)
            o = o * lax.rsqrt(ms + SUBLN_EPS) * sw * (1.0 - LAMBDA_INIT)
            o_ref[0, qs, r * LANES:(r + 1) * LANES] = o.astype(o_ref.dtype)
        return carry

    lax.fori_loop(0, seq // tq, q_block, 0)


def _attn_prompt(q, k, v, lam_vecs, subln_w, tq=256):
    b, seq, _ = q.shape
    kv_spec = pl.BlockSpec((None, seq * N_KV_HEADS, HEAD_W), lambda i, h: (i, 0, 0))
    return pl.pallas_call(
        functools.partial(_attn_prompt_kernel, tq=tq, seq=seq),
        grid=(b, N_KV_HEADS),
        in_specs=[_resident(lam_vecs.shape),
                  pl.BlockSpec((1, seq, 2 * HEAD_W), lambda i, h: (i, 0, h)),
                  kv_spec, kv_spec, _resident((1, HEAD_W))],
        out_specs=pl.BlockSpec((1, seq, 2 * HEAD_W), lambda i, h: (i, 0, h)),
        out_shape=jax.ShapeDtypeStruct((b, seq, ATT_WIDTH), BF16),
        scratch_shapes=[pltpu.VMEM((N_KV_HEADS, seq, HEAD_W), BF16),
                        pltpu.VMEM((N_KV_HEADS, HEAD_W + bf16_tile_rows, seq), BF16),
                        pltpu.VMEM((HEAD_W, 4 * tq), BF16), pltpu.VMEM((1, 4 * tq), F32),
                        pltpu.VMEM((HEAD_W + bf16_tile_rows, 4 * tq), F32),
                        pltpu.VMEM((tq, 4 * tq), F32), pltpu.VMEM((1, 4 * tq), F32),
                        pltpu.VMEM((tq, 4 * tq), F32), pltpu.VMEM((1, 4 * tq), F32)],
        compiler_params=pltpu.CompilerParams(dimension_semantics=("parallel", "arbitrary"),
                                             vmem_limit_bytes=VMEM_LIMIT),
        name="attn_prompt",
    )(lam_vecs, q, k, v, subln_w)


def _col_broadcast(row):
    return jnp.broadcast_to(row, (LANES, LANES)).T


def _attn_paged_kernel(pt_ref, lam_ref, q_ref, kn_ref, vn_ref, sw_ref, ck_hbm, cv_hbm, o_ref, kbuf, vbuf, sem,
                       qall_sc, sa_sc, smaxa_sc, sb_sc, smaxb_sc, m_sc, l_sc, acc_sc, *,
                       pages_per_group, n_groups, n_batch, t_new):
    b = pl.program_id(0)
    n_slots = kbuf.shape[0]
    lookahead = n_slots - 2
    n_pages = pages_per_group * n_groups
    rows_pg = PAGE_SIZE * N_KV_HEADS
    head_shift = int(math.log2(N_KV_HEADS))
    col_shift = int(math.log2(4 * t_new))
    row_i = lax.broadcasted_iota(jnp.int32, (rows_pg, LANES), 0)
    col_i = lax.broadcasted_iota(jnp.int32, (rows_pg, LANES), 1)
    head_match = (row_i & (N_KV_HEADS - 1)) == (col_i >> col_shift)

    def group_copies(batch, g):
        slot = g % n_slots
        copies = []
        for i in range(pages_per_group):
            page = pt_ref[batch * n_pages + g * pages_per_group + i]
            dst = pl.ds(i * rows_pg, rows_pg)
            copies.append(pltpu.make_async_copy(ck_hbm.at[page], kbuf.at[slot, dst], sem.at[0, slot]))
            copies.append(pltpu.make_async_copy(cv_hbm.at[page], vbuf.at[slot, dst], sem.at[1, slot]))
        return copies

    def start_group(batch, g):
        for cp in group_copies(batch, g):
            cp.start()

    def wait_group(batch, g):
        for cp in group_copies(batch, g):
            cp.wait()

    @pl.when(b == 0)
    def _():
        for g in range(lookahead):
            start_group(b, g)

    lane_lo = lax.broadcasted_iota(jnp.int32, (t_new, LANES), 1) < ATT_HEAD_DIM
    pieces = []
    for hr in range(2 * N_KV_HEADS):
        pair = q_ref[0, :, hr * HEAD_W:(hr + 1) * HEAD_W].astype(F32)
        zero = jnp.zeros_like(pair)
        pieces += [jnp.where(lane_lo, pair, zero), jnp.where(lane_lo, zero, pair)]
    qall_sc[...] = jnp.concatenate(pieces, axis=0).T.astype(BF16)
    m_sc[...] = jnp.full_like(m_sc, -jnp.inf)
    l_sc[...] = jnp.zeros_like(l_sc)
    acc_sc[...] = jnp.zeros_like(acc_sc)

    def stage_scores(stage, n_blocks, rows, score_fn):
        s_ref, smax_ref = stage
        smax = jnp.full((1, LANES), -jnp.inf, F32)
        for i in range(n_blocks):
            s = score_fn(i)
            s_ref[i * rows_pg:i * rows_pg + rows, :] = s
            smax = jnp.maximum(smax, jnp.max(s, axis=0, keepdims=True))
        smax_ref[...] = smax

    def consume(stage, n_blocks, rows, value_fn):
        s_ref, smax_ref = stage
        m_prev = m_sc[...]
        m_new = jnp.maximum(m_prev, smax_ref[...])
        alpha = jnp.exp2(m_prev - m_new)
        lsum = jnp.zeros_like(m_new)
        pv = jnp.zeros((LANES, HEAD_W), F32)
        for i in range(n_blocks):
            p = jnp.exp2(s_ref[i * rows_pg:i * rows_pg + rows, :] - m_new)
            lsum = lsum + jnp.sum(p, axis=0, keepdims=True)
            pv = pv + jnp.dot(p.T.astype(BF16), value_fn(i), preferred_element_type=F32)
        l_sc[...] = alpha * l_sc[...] + lsum
        acc_sc[...] = _col_broadcast(alpha) * acc_sc[...] + pv
        m_sc[...] = m_new

    stages = ((sa_sc, smaxa_sc), (sb_sc, smaxb_sc))

    def group_scores(g):
        slot = g % n_slots

        def page_scores(i):
            page = kbuf[slot, i * rows_pg:(i + 1) * rows_pg, :].astype(BF16)
            return jnp.where(head_match, jnp.dot(page, qall_sc[...], preferred_element_type=F32), -jnp.inf)

        stage_scores(stages[g % 2], pages_per_group, rows_pg, page_scores)

    def group_consume(g):
        slot = g % n_slots
        consume(stages[g % 2], pages_per_group, rows_pg,
                lambda i: vbuf[slot, i * rows_pg:(i + 1) * rows_pg, :].astype(BF16))

    for g in range(n_groups):
        wait_group(b, g)
        nxt = g + lookahead
        if nxt < n_groups:
            start_group(b, nxt)
        else:
            @pl.when(b + 1 < n_batch)
            def _():
                start_group(b + 1, nxt - n_groups)
        group_scores(g)
        if g >= 1:
            group_consume(g - 1)
    group_consume(n_groups - 1)

    pad = jnp.zeros((LANES - t_new * N_KV_HEADS, HEAD_W), F32)
    kn = jnp.concatenate([kn_ref[0], pad], axis=0).astype(BF16)
    vn = jnp.concatenate([vn_ref[0], pad], axis=0).astype(BF16)
    row_n = lax.broadcasted_iota(jnp.int32, (LANES, LANES), 0)
    col_n = lax.broadcasted_iota(jnp.int32, (LANES, LANES), 1)
    live = (((row_n & (N_KV_HEADS - 1)) == (col_n >> col_shift))
            & ((row_n >> head_shift) <= (col_n & (t_new - 1))))
    s_new = jnp.where(live, jnp.dot(kn, qall_sc[...], preferred_element_type=F32), -jnp.inf)
    stage_scores(stages[0], 1, LANES, lambda i: s_new)
    consume(stages[0], 1, LANES, lambda i: vn)
    lam = _lambda(lam_ref)
    sw = sw_ref[...]
    on = acc_sc[...] / _col_broadcast(l_sc[...])
    for hr in range(2 * N_KV_HEADS):
        base = hr * 2 * t_new
        o = on[base:base + t_new, :] - lam * on[base + t_new:base + 2 * t_new, :]
        ms = jnp.mean(o * o, axis=-1, keepdims=True)
        o = o * lax.rsqrt(ms + SUBLN_EPS) * sw * (1.0 - LAMBDA_INIT)
        o_ref[0, :, hr * HEAD_W:(hr + 1) * HEAD_W] = o.astype(o_ref.dtype)


def _attn_paged(q, k_new, v_new, cache_k, cache_v, page_table, lam_vecs, subln_w, pages_per_group=4, n_slots=4):
    b, t_new, _ = q.shape
    n_pages = page_table.shape[1]
    n_groups = n_pages // pages_per_group
    rows_pg = PAGE_SIZE * N_KV_HEADS
    assert N_KV_HEADS * 4 * t_new == LANES
    assert n_groups % n_slots == 0 and n_groups >= n_slots
    pt_flat = page_table.reshape(-1)

    per_b = lambda r, w: pl.BlockSpec((1, r, w), lambda bi, pt: (bi, 0, 0))
    const = lambda shape: pl.BlockSpec(shape, lambda bi, pt: (0,) * len(shape))
    hbm = pl.BlockSpec(memory_space=pl.ANY)
    group_rows = pages_per_group * rows_pg
    grid_spec = pltpu.PrefetchScalarGridSpec(
        num_scalar_prefetch=1,
        grid=(b,),
        in_specs=[const(lam_vecs.shape), per_b(t_new, ATT_WIDTH), per_b(t_new * N_KV_HEADS, HEAD_W),
                  per_b(t_new * N_KV_HEADS, HEAD_W), const((1, HEAD_W)), hbm, hbm],
        out_specs=per_b(t_new, ATT_WIDTH),
        scratch_shapes=[pltpu.VMEM((n_slots, group_rows, HEAD_W), F32), pltpu.VMEM((n_slots, group_rows, HEAD_W), F32),
                        pltpu.SemaphoreType.DMA((2, n_slots)),
                        pltpu.VMEM((HEAD_W, LANES), BF16),
                        pltpu.VMEM((group_rows, LANES), F32), pltpu.VMEM((1, LANES), F32),
                        pltpu.VMEM((group_rows, LANES), F32), pltpu.VMEM((1, LANES), F32),
                        pltpu.VMEM((1, LANES), F32), pltpu.VMEM((1, LANES), F32), pltpu.VMEM((LANES, HEAD_W), F32)],
    )
    return pl.pallas_call(
        functools.partial(_attn_paged_kernel, pages_per_group=pages_per_group, n_groups=n_groups, n_batch=b,
                          t_new=t_new),
        grid_spec=grid_spec,
        out_shape=jax.ShapeDtypeStruct((b, t_new, ATT_WIDTH), BF16),
        compiler_params=pltpu.CompilerParams(dimension_semantics=("arbitrary",),
                                             vmem_limit_bytes=VMEM_LIMIT),
        name="attn_paged",
    )(pt_flat, lam_vecs, q, k_new, v_new, subln_w, cache_k, cache_v)


def _mix_kernel(y_ref, z_ref, o_ref, g_ref, x_ref, nw_ref, wso_ref, wao_ref, wo_ref, out_ref):
    y = y_ref[...].astype(F32) * _silu_tanh(z_ref[...].astype(F32))
    gw = D_INNER // N_SSM_GROUPS
    parts = []
    for g in range(N_SSM_GROUPS):
        yg = y[:, g * gw:(g + 1) * gw]
        ms = jnp.mean(yg * yg, axis=-1, keepdims=True)
        parts.append((yg * lax.rsqrt(ms + NORM_EPS) * nw_ref[:, g * gw:(g + 1) * gw]).astype(BF16))
    y_ssm = jnp.dot(jnp.concatenate(parts, axis=1), wso_ref[...], preferred_element_type=F32)
    y_att = jnp.dot(o_ref[...], wao_ref[...], preferred_element_type=F32)
    gates = 0.5 * jnp.tanh(0.5 * g_ref[...].astype(F32)) + 0.5
    mixed = (gates[:, :D_MODEL] * y_ssm + gates[:, D_MODEL:] * y_att).astype(BF16)
    out_ref[...] = x_ref[...] + jnp.dot(mixed, wo_ref[...], preferred_element_type=F32)


def _mix(y, z, o, gates, x, ssm_norm_w, w_ssm_out, w_att_out, w_o, tm=512):
    n = x.shape[0]
    row = lambda w: pl.BlockSpec((tm, w), lambda i: (i, 0))
    return pl.pallas_call(
        _mix_kernel,
        grid=(n // tm,),
        in_specs=[row(D_INNER), row(D_INNER), row(ATT_WIDTH), row(2 * D_MODEL), row(D_MODEL),
                  _resident((1, D_INNER)), _resident((D_INNER, D_MODEL)), _resident((ATT_WIDTH, D_MODEL)),
                  _resident((D_MODEL, D_MODEL))],
        out_specs=row(D_MODEL),
        out_shape=jax.ShapeDtypeStruct((n, D_MODEL), F32),
        compiler_params=pltpu.CompilerParams(dimension_semantics=("parallel",), vmem_limit_bytes=VMEM_LIMIT),
        name="mix",
    )(y, z, o, gates, x, ssm_norm_w, w_ssm_out, w_att_out, w_o)


def _ffn_kernel(x_ref, nw_ref, win_ref, wout_ref, fw_ref, out_ref, *, chunk):
    x = x_ref[...]
    ms = jnp.mean(x * x, axis=-1, keepdims=True)
    h = (x * lax.rsqrt(ms + NORM_EPS) * nw_ref[...]).astype(BF16)
    acc = x
    for c in range(0, D_FF, chunk):
        gate = jnp.dot(h, win_ref[:, c:c + chunk], preferred_element_type=F32)
        up = jnp.dot(h, win_ref[:, D_FF + c:D_FF + c + chunk], preferred_element_type=F32)
        hid = (_silu_tanh(gate) * up).astype(BF16)
        acc = acc + jnp.dot(hid, wout_ref[c:c + chunk, :], preferred_element_type=F32)
    ms2 = jnp.mean(acc * acc, axis=-1, keepdims=True)
    out_ref[...] = acc * lax.rsqrt(ms2 + NORM_EPS) * fw_ref[...]


def _ffn(x, norm_w, w_in, w_out, final_w, tm=1024, chunk=256):
    n = x.shape[0]
    tm = min(tm, n)
    row = pl.BlockSpec((tm, D_MODEL), lambda i: (i, 0))
    return pl.pallas_call(
        functools.partial(_ffn_kernel, chunk=chunk),
        grid=(n // tm,),
        in_specs=[row, _resident((1, D_MODEL)), _resident((D_MODEL, 2 * D_FF)), _resident((D_FF, D_MODEL)),
                  _resident((1, D_MODEL))],
        out_specs=row,
        out_shape=jax.ShapeDtypeStruct((n, D_MODEL), F32),
        compiler_params=pltpu.CompilerParams(dimension_semantics=("parallel",), vmem_limit_bytes=VMEM_LIMIT),
        name="ffn",
    )(x, norm_w, w_in, w_out, final_w)


def _pack_w_in(w_in):
    cuts = np.cumsum((D_INNER, CONV_DIM, N_SSM_HEADS, ATT_WIDTH, KV_WIDTH, KV_WIDTH, 2 * D_MODEL))[:-1]
    wz, wxbc, wdt, wq, wk, wv, wg = jnp.split(w_in, [int(c) for c in cuts], axis=-1)
    wdt = jnp.pad(wdt, ((0, 0), (0, DT_PAD - N_SSM_HEADS)))
    wq = wq * (ATT_HEAD_DIM ** -0.5 * math.log2(math.e))
    return jnp.concatenate([wz, wxbc, wq, wk, wv, wg, wdt], axis=-1).astype(BF16)


def kernel(x_prompt, x_sample, cache_k, cache_v, state_conv, state_ssm, page_table, norm_mix_w, w_in, conv_w, conv_b, dt_bias, a_log, d_skip, ssm_norm_w, w_ssm_out, lambda_q1, lambda_k1, lambda_q2, lambda_k2, subln_w, w_att_out, w_o, norm_ffn_w, w_ffn_in, w_ffn_out, norm_final_w):
    l = 0
    w_all = _pack_w_in(w_in[l])
    pad_heads = lambda t: jnp.pad(t.reshape(1, N_SSM_HEADS), ((0, 0), (0, DT_PAD - N_SSM_HEADS)))
    lam_vecs = jnp.stack([lambda_q1[l], lambda_k1[l], lambda_q2[l], lambda_k2[l]])
    params = dict(
        gain=norm_mix_w[l].reshape(1, D_MODEL), conv_w=conv_w[l], conv_b=conv_b[l].reshape(1, CONV_DIM),
        dt_bias=pad_heads(dt_bias[l]), a_log=pad_heads(a_log[l]),
        d_skip_x=jnp.repeat(d_skip[l], SSM_HEAD_DIM).reshape(1, D_INNER),
        ssm_norm_w=ssm_norm_w[l].reshape(1, D_INNER), w_ssm_out=w_ssm_out[l].astype(BF16),
        subln_w=subln_w[l].reshape(1, 2 * ATT_HEAD_DIM), w_att_out=w_att_out[l].astype(BF16),
        w_o=w_o[l].astype(BF16), norm_ffn_w=norm_ffn_w[l].reshape(1, D_MODEL),
        w_ffn_in=w_ffn_in[l].astype(BF16), w_ffn_out=w_ffn_out[l].astype(BF16),
        norm_final_w=norm_final_w.reshape(1, D_MODEL))

    def group(x, conv0, h0, attend):
        b, seq, _ = x.shape
        n = b * seq
        x2 = x.reshape(n, D_MODEL)
        z, xbc, q, k, v, gates, dt = _inproj(x2, params["gain"], w_all[:, :SEG_Q], w_all[:, SEG_Q:])
        y, conv_state, ssm_state = _ssd(
            xbc.reshape(b, seq, CONV_DIM), dt.reshape(b, seq, DT_PAD), conv0, h0.reshape(b, D_INNER, D_STATE),
            params["conv_w"], params["conv_b"], params["dt_bias"], params["a_log"], params["d_skip_x"])
        k3 = k.reshape(b, seq * N_KV_HEADS, HEAD_W)
        v3 = v.reshape(b, seq * N_KV_HEADS, HEAD_W)
        o = attend(q.reshape(b, seq, ATT_WIDTH), k3, v3)
        x1 = _mix(y.reshape(n, D_INNER), z, o.reshape(n, ATT_WIDTH), gates, x2, params["ssm_norm_w"],
                  params["w_ssm_out"], params["w_att_out"], params["w_o"])
        out = _ffn(x1, params["norm_ffn_w"], params["w_ffn_in"], params["w_ffn_out"], params["norm_final_w"])
        kv_shape = (1, b, seq, N_KV_HEADS, 2 * ATT_HEAD_DIM)
        return (out.reshape(b, seq, D_MODEL), k.reshape(kv_shape), v.reshape(kv_shape), conv_state[None],
                ssm_state.reshape(1, b, N_SSM_HEADS, SSM_HEAD_DIM, D_STATE))

    bp = x_prompt.shape[0]
    attend_p = lambda q, k, v: _attn_prompt(q, k, v, lam_vecs, params["subln_w"])
    yp, kp, vp, cp, hp = group(x_prompt, jnp.zeros((bp, D_CONV - 1, CONV_DIM), F32),
                               jnp.zeros((bp, N_SSM_HEADS, SSM_HEAD_DIM, D_STATE), F32), attend_p)
    n_pool = cache_k.shape[1]
    ck = cache_k[l].reshape(n_pool, PAGE_SIZE * N_KV_HEADS, HEAD_W)
    cv = cache_v[l].reshape(n_pool, PAGE_SIZE * N_KV_HEADS, HEAD_W)
    attend_s = lambda q, k, v: _attn_paged(q, k, v, ck, cv, page_table, lam_vecs, params["subln_w"])
    ys, ks, vs, cs, hs = group(x_sample, state_conv[l], state_ssm[l], attend_s)
    return (yp, ys, kp, vp, cp, hp, ks, vs, cs, hs)
```

```python
import functools
import math

import jax
import jax.numpy as jnp
import numpy as np
from jax import lax
from jax.experimental import pallas as pl
from jax.experimental.pallas import tpu as pltpu

F32 = jnp.float32
BF16 = jnp.bfloat16

D_MODEL = 1024
D_INNER = 2048
SSM_HEAD_DIM = 64
N_SSM_HEADS = 32
N_SSM_GROUPS = 4
D_STATE = 128
D_CONV = 4
CONV_DIM = 3072
ATT_HEAD_DIM = 64
N_KV_HEADS = 4
ATT_WIDTH = 1024
KV_WIDTH = 512
D_FF = 2816
PAGE_SIZE = 128
HEAD_W = 2 * ATT_HEAD_DIM
NORM_EPS = 1e-6
SUBLN_EPS = 1e-5
LAMBDA_INIT = 0.8 - 0.6 * math.exp(-0.3 * 0)

LANES = 128
bf16_tile_rows = 16
CHUNK = 128
DT_PAD = LANES
VMEM_LIMIT = 56 * 1024 * 1024

SEG_Z, SEG_XBC, SEG_Q, SEG_K, SEG_V, SEG_G, SEG_DT = 0, 2048, 5120, 6144, 6656, 7168, 9216
W_ALL = SEG_DT + DT_PAD


def _resident(shape):
    nd = len(shape)
    return pl.BlockSpec(shape, lambda *_: (0,) * nd, pipeline_mode=pl.Buffered(1))


def _silu_tanh(x):
    h = 0.5 * x
    return h + h * jnp.tanh(h)


def _inproj_kernel(x_ref, g_ref, w_ref, *out_refs, segs):
    x = x_ref[...]
    tm = x_ref.shape[0]
    ms = jnp.mean(x * x, axis=-1, keepdims=True)
    h = (x * lax.rsqrt(ms + NORM_EPS) * g_ref[...]).astype(BF16)
    for ref, (start, width, _, kv_rows) in zip(out_refs, segs):
        step = min(width, 512)
        for c in range(0, width, step):
            r = jnp.dot(h, w_ref[:, start + c:start + c + step], preferred_element_type=F32)
            if kv_rows:
                for hd in range(c // HEAD_W, (c + step) // HEAD_W):
                    ref[pl.ds(hd, tm, stride=N_KV_HEADS), :] = r[:, hd * HEAD_W - c:(hd + 1) * HEAD_W - c]
            else:
                ref[:, c:c + step] = r.astype(ref.dtype)


def _inproj_call(x2d, gain, w, segs, tm):
    n = x2d.shape[0]
    specs, shapes = [], []
    for _, width, dtype, kv_rows in segs:
        if kv_rows:
            specs.append(pl.BlockSpec((tm * N_KV_HEADS, HEAD_W), lambda i: (i, 0)))
            shapes.append(jax.ShapeDtypeStruct((n * N_KV_HEADS, HEAD_W), dtype))
        else:
            specs.append(pl.BlockSpec((tm, width), lambda i: (i, 0)))
            shapes.append(jax.ShapeDtypeStruct((n, width), dtype))
    return pl.pallas_call(
        functools.partial(_inproj_kernel, segs=segs),
        grid=(n // tm,),
        in_specs=[pl.BlockSpec((tm, D_MODEL), lambda i: (i, 0)), _resident((1, D_MODEL)), _resident(w.shape)],
        out_specs=specs,
        out_shape=shapes,
        compiler_params=pltpu.CompilerParams(dimension_semantics=("parallel",), vmem_limit_bytes=VMEM_LIMIT),
        name="in_proj",
    )(x2d, gain, w)


def _inproj(x2d, gain, w_all, tm=512):
    segs = ((SEG_Z, D_INNER, BF16, False), (SEG_XBC, CONV_DIM, BF16, False), (SEG_Q, ATT_WIDTH, BF16, False),
            (SEG_K, KV_WIDTH, F32, True), (SEG_V, KV_WIDTH, F32, True), (SEG_G, 2 * D_MODEL, BF16, False),
            (SEG_DT, DT_PAD, F32, False))
    return _inproj_call(x2d, gain, w_all, segs, tm)


def _split3_packed(v):
    a = v.astype(BF16).astype(F32)
    r = v - a
    b = r.astype(BF16).astype(F32)
    c = (r - b).astype(BF16).astype(F32)
    packed = a + pltpu.roll(b, N_SSM_HEADS, axis=1) + pltpu.roll(c, 2 * N_SSM_HEADS, axis=1)
    return packed.astype(BF16)


def _ssd_kernel(xbc_ref, dt_ref, pre_ref, st0_ref, cw_ref, cb_ref, dtb_ref, alog_ref, dsk_ref, e64_ref, e128_ref,
                y_ref, conv_ref, st_ref, xp_sc, act_sc, st_sc, cst_sc, *, t_in, n_chunks):
    c = pl.program_id(1)
    T = CHUNK
    row_pad = 8
    live = T if t_in == T else max(t_in, bf16_tile_rows)

    @pl.when(c == 0)
    def _():
        xp_sc[...] = jnp.zeros_like(xp_sc)
        xp_sc[row_pad - (D_CONV - 1):row_pad, :] = pre_ref[0]
        if live < T:
            act_sc[live:, :] = jnp.zeros((T - live, CONV_DIM), F32)
        for blk in range(D_INNER // LANES):
            st_sc[:, blk * LANES:(blk + 1) * LANES] = st0_ref[0, blk * LANES:(blk + 1) * LANES, :].T

    xp_sc[row_pad:row_pad + t_in, :] = xbc_ref[0].astype(F32)

    cblk = 256
    for j in range(CONV_DIM // cblk):
        cs_ = slice(j * cblk, (j + 1) * cblk)
        window = xp_sc[0:row_pad + live, cs_]
        acc = cb_ref[:, cs_] + window[row_pad:] * cw_ref[D_CONV - 1:D_CONV, cs_]
        for k in range(D_CONV - 1):
            shifted = pltpu.roll(window, D_CONV - 1 - k, axis=0)[row_pad:]
            acc = acc + shifted * cw_ref[k:k + 1, cs_]
        act_sc[0:live, cs_] = _silu_tanh(acc)

    lane = lax.broadcasted_iota(jnp.int32, (T, LANES), 1)
    rowi = lax.broadcasted_iota(jnp.int32, (T, LANES), 0)
    if t_in < T:
        dt_raw = jnp.concatenate([dt_ref[0], jnp.zeros((T - t_in, DT_PAD), F32)], axis=0)
    else:
        dt_raw = dt_ref[0]
    v = dt_raw + dtb_ref[...]
    softplus = jnp.maximum(v, 0.0) + jnp.log1p(jnp.exp(-jnp.abs(v)))
    dt = jnp.where((lane < N_SSM_HEADS) & (rowi < t_in), softplus, 0.0)
    la = dt * (-jnp.exp(alog_ref[...]))
    cs = la
    d = 1
    while d < T:
        cs = cs + jnp.where(rowi >= d, pltpu.roll(cs, d, axis=0), 0.0)
        d *= 2
    cs = cs * math.log2(math.e)
    cst_sc[...] = cs.T
    x_dt = _split3_packed(dt)
    x_cs = _split3_packed(cs)
    x_cs_live = _split3_packed(cs[0:live])
    causal = (lax.broadcasted_iota(jnp.int32, (live, LANES), 0) >= lax.broadcasted_iota(jnp.int32, (live, LANES), 1))
    lo_half = lane < SSM_HEAD_DIM

    heads_per_group = N_SSM_HEADS // N_SSM_GROUPS
    gw = heads_per_group * SSM_HEAD_DIM
    for g in range(N_SSM_GROUPS):
        c_col = D_INNER + N_SSM_GROUPS * D_STATE + g * D_STATE
        bm = act_sc[:, D_INNER + g * D_STATE:D_INNER + (g + 1) * D_STATE]
        bm_b = bm.astype(BF16)
        cm_b = act_sc[0:live, c_col:c_col + D_STATE].astype(BF16)
        bmt_b = bm.T.astype(BF16)
        cb = lax.dot_general(cm_b, bm_b, (((1,), (1,)), ((), ())), preferred_element_type=F32)
        gs = slice(g * gw, (g + 1) * gw)
        dtx = jnp.dot(x_dt, e64_ref[:, gs], preferred_element_type=F32)
        csx = jnp.dot(x_cs, e64_ref[:, gs], preferred_element_type=F32)
        cs_last = csx[T - 1:T, :]
        xs = act_sc[:, gs]
        xd = xs * dtx
        xdd_b = (xd * jnp.exp2(cs_last - csx)).astype(BF16)
        xd_b = xd.astype(BF16)
        st_old = st_sc[:, gs]
        y_off = jnp.dot(cm_b, st_old.astype(BF16), preferred_element_type=F32) * jnp.exp2(csx[0:live])
        st_sc[:, gs] = st_old * jnp.exp2(cs_last) + jnp.dot(bmt_b, xdd_b, preferred_element_type=F32)
        y_skip = xs[0:live] * dsk_ref[:, gs]
        for pr in range(heads_per_group // 2):
            ms = []
            for hh in range(2):
                h = g * heads_per_group + pr * 2 + hh
                cs_col = jnp.dot(x_cs_live, e128_ref[:, h * LANES:(h + 1) * LANES], preferred_element_type=F32)
                seg = jnp.where(causal, cs_col - cst_sc[h:h + 1, :], -jnp.inf)
                ms.append((cb * jnp.exp2(seg)).astype(BF16))
            ps = slice(pr * LANES, (pr + 1) * LANES)
            xd_p = xd_b[:, ps]
            zero = jnp.zeros_like(xd_p)
            rhs = jnp.concatenate([jnp.where(lo_half, xd_p, zero), jnp.where(lo_half, zero, xd_p)], axis=0)
            y_diag = jnp.dot(jnp.concatenate(ms, axis=1), rhs, preferred_element_type=F32)
            y = y_diag + y_off[:, ps] + y_skip[:, ps]
            y_ref[0, :, g * gw + pr * LANES:g * gw + (pr + 1) * LANES] = y[:t_in].astype(y_ref.dtype)

    @pl.when(c == n_chunks - 1)
    def _():
        conv_ref[0] = xp_sc[row_pad + t_in - (D_CONV - 1):row_pad + t_in, :]
        for blk in range(D_INNER // LANES):
            st_ref[0, blk * LANES:(blk + 1) * LANES, :] = st_sc[:, blk * LANES:(blk + 1) * LANES].T

    if n_chunks > 1:
        xp_sc[0:row_pad, :] = xp_sc[T:T + row_pad, :]


def _expansion_matrices():
    e64 = np.zeros((LANES, D_INNER), np.float32)
    e128 = np.zeros((LANES, N_SSM_HEADS * LANES), np.float32)
    for part in range(3):
        for h in range(N_SSM_HEADS):
            e64[part * N_SSM_HEADS + h, h * SSM_HEAD_DIM:(h + 1) * SSM_HEAD_DIM] = 1.0
            e128[part * N_SSM_HEADS + h, h * LANES:(h + 1) * LANES] = 1.0
    return jnp.asarray(e64, BF16), jnp.asarray(e128, BF16)


def _ssd(xbc, dt, prefix, st0, conv_w, conv_b, dt_bias, a_log, d_skip_x):
    b, l, _ = xbc.shape
    t_in = min(l, CHUNK)
    n_chunks = l // t_in
    e64, e128 = _expansion_matrices()
    per_b = lambda shape: pl.BlockSpec((1,) + shape, lambda i, c: (i, 0, 0))
    per_chunk = lambda w: pl.BlockSpec((1, t_in, w), lambda i, c: (i, c, 0))
    return pl.pallas_call(
        functools.partial(_ssd_kernel, t_in=t_in, n_chunks=n_chunks),
        grid=(b, n_chunks),
        in_specs=[per_chunk(CONV_DIM), per_chunk(DT_PAD), per_b((D_CONV - 1, CONV_DIM)), per_b((D_INNER, D_STATE)),
                  _resident((D_CONV, CONV_DIM)), _resident((1, CONV_DIM)), _resident((1, DT_PAD)),
                  _resident((1, DT_PAD)), _resident((1, D_INNER)), _resident(e64.shape), _resident(e128.shape)],
        out_specs=[per_chunk(D_INNER), per_b((D_CONV - 1, CONV_DIM)), per_b((D_INNER, D_STATE))],
        out_shape=[jax.ShapeDtypeStruct((b, l, D_INNER), BF16),
                   jax.ShapeDtypeStruct((b, D_CONV - 1, CONV_DIM), F32),
                   jax.ShapeDtypeStruct((b, D_INNER, D_STATE), F32)],
        scratch_shapes=[pltpu.VMEM((8 + CHUNK, CONV_DIM), F32), pltpu.VMEM((CHUNK, CONV_DIM), F32),
                        pltpu.VMEM((D_STATE, D_INNER), F32), pltpu.VMEM((LANES, CHUNK), F32)],
        compiler_params=pltpu.CompilerParams(dimension_semantics=("parallel", "arbitrary"),
                                             vmem_limit_bytes=VMEM_LIMIT),
        name="ssd",
    )(xbc, dt, prefix, st0, conv_w, conv_b, dt_bias, a_log, d_skip_x, e64, e128)


def _lambda(lam_ref):
    lv = lam_ref[...]
    s1 = jnp.sum(lv[0:1] * lv[1:2], axis=-1, keepdims=True)
    s2 = jnp.sum(lv[2:3] * lv[3:4], axis=-1, keepdims=True)
    return jnp.exp(s1) - jnp.exp(s2) + LAMBDA_INIT


def _attn_prompt_kernel(lam_ref, q_ref, k_ref, v_ref, sw_ref, o_ref, kall_sc, vtall_sc, qt_sc, m_sc, acc_sc,
                        sa_sc, smaxa_sc, sb_sc, smaxb_sc, *, tq, seq):
    h = pl.program_id(1)
    extra = vtall_sc.shape[1] - HEAD_W

    @pl.when(h == 0)
    def _():
        ones_row = (lax.broadcasted_iota(jnp.int32, (extra, seq), 0) == 0).astype(F32).astype(BF16)
        for hd in range(N_KV_HEADS):
            kall_sc[hd] = k_ref[pl.ds(hd, seq, stride=N_KV_HEADS), :].astype(BF16)
            for blk in range(seq // LANES):
                vb = v_ref[pl.ds(hd + blk * LANES * N_KV_HEADS, LANES, stride=N_KV_HEADS), :]
                vtall_sc[hd, 0:HEAD_W, blk * LANES:(blk + 1) * LANES] = vb.T.astype(BF16)
            vtall_sc[hd, HEAD_W:, :] = ones_row

    k_sc = kall_sc.at[h]
    vt_sc = vtall_sc.at[h]
    lam = _lambda(lam_ref)
    sw = sw_ref[...]
    feat_lo = lax.broadcasted_iota(jnp.int32, (LANES, tq), 0) < ATT_HEAD_DIM

    def kv_slice(start):
        return pl.ds(pl.multiple_of(start, tq), tq)

    def scores(start, stage, causal=None):
        s_ref, smax_ref = stage
        s = jnp.dot(k_sc[kv_slice(start), :], qt_sc[...], preferred_element_type=F32)
        if causal is not None:
            s = jnp.where(causal, s, -jnp.inf)
        s_ref[...] = s
        smax_ref[...] = jnp.max(s, axis=0, keepdims=True)

    def consume(start, stage):
        s_ref, smax_ref = stage
        m_prev = m_sc[...]
        m_new = jnp.maximum(m_prev, smax_ref[...])
        alpha = jnp.exp2(m_prev - m_new)
        p = jnp.exp2(s_ref[...] - m_new)
        pv = jnp.dot(vt_sc[:, kv_slice(start)], p.astype(BF16), preferred_element_type=F32)
        acc_sc[...] = alpha * acc_sc[...] + pv
        m_sc[...] = m_new

    stage_a, stage_b = (sa_sc, smaxa_sc), (sb_sc, smaxb_sc)

    def q_block(qi, carry):
        qs = pl.ds(pl.multiple_of(qi * tq, tq), tq)
        for r in range(2):
            pair_t = q_ref[0, qs, r * LANES:(r + 1) * LANES].astype(F32).T
            zero = jnp.zeros_like(pair_t)
            qt_sc[:, (2 * r) * tq:(2 * r + 1) * tq] = jnp.where(feat_lo, pair_t, zero).astype(BF16)
            qt_sc[:, (2 * r + 1) * tq:(2 * r + 2) * tq] = jnp.where(feat_lo, zero, pair_t).astype(BF16)
        m_sc[...] = jnp.full_like(m_sc, -jnp.inf)
        acc_sc[...] = jnp.zeros_like(acc_sc)

        key_i = lax.broadcasted_iota(jnp.int32, (tq, 4 * tq), 0)
        qt_i = lax.broadcasted_iota(jnp.int32, (tq, 4 * tq), 1) & (tq - 1)
        scores(qi * tq, stage_a, causal=key_i <= qt_i)

        def kv_pair(t, start_in_a):
            scores(2 * t * tq, stage_b)
            consume(start_in_a, stage_a)
            scores((2 * t + 1) * tq, stage_a)
            consume(2 * t * tq, stage_b)
            return (2 * t + 1) * tq

        start_in_a = lax.fori_loop(0, qi // 2, kv_pair, qi * tq)

        @pl.when(qi % 2 == 1)
        def _():
            scores((qi - 1) * tq, stage_b)
            consume(start_in_a, stage_a)
            consume((qi - 1) * tq, stage_b)

        @pl.when(qi % 2 == 0)
        def _():
            consume(start_in_a, stage_a)

        on = acc_sc[0:HEAD_W, :] / acc_sc[HEAD_W:HEAD_W + 1, :]
        for r in range(2):
            o_t = on[:, (2 * r) * tq:(2 * r + 1) * tq] - lam * on[:, (2 * r + 1) * tq:(2 * r + 2) * tq]
            o = o_t.T
            ms = jnp.mean(o * o, axis=-1, keepdims=True)
            o = o * lax.rsqrt(ms + SUBLN_EPS) * sw * (1.0 - LAMBDA_INIT)
            o_ref[0, qs, r * LANES:(r + 1) * LANES] = o.astype(o_ref.dtype)
        return carry

    lax.fori_loop(0, seq // tq, q_block, 0)


def _attn_prompt(q, k, v, lam_vecs, subln_w, tq=256):
    b, seq, _ = q.shape
    kv_spec = pl.BlockSpec((None, seq * N_KV_HEADS, HEAD_W), lambda i, h: (i, 0, 0))
    return pl.pallas_call(
        functools.partial(_attn_prompt_kernel, tq=tq, seq=seq),
        grid=(b, N_KV_HEADS),
        in_specs=[_resident(lam_vecs.shape),
                  pl.BlockSpec((1, seq, 2 * HEAD_W), lambda i, h: (i, 0, h)),
                  kv_spec, kv_spec, _resident((1, HEAD_W))],
        out_specs=pl.BlockSpec((1, seq, 2 * HEAD_W), lambda i, h: (i, 0, h)),
        out_shape=jax.ShapeDtypeStruct((b, seq, ATT_WIDTH), BF16),
        scratch_shapes=[pltpu.VMEM((N_KV_HEADS, seq, HEAD_W), BF16),
                        pltpu.VMEM((N_KV_HEADS, HEAD_W + bf16_tile_rows, seq), BF16),
                        pltpu.VMEM((HEAD_W, 4 * tq), BF16), pltpu.VMEM((1, 4 * tq), F32),
                        pltpu.VMEM((HEAD_W + bf16_tile_rows, 4 * tq), F32),
                        pltpu.VMEM((tq, 4 * tq), F32), pltpu.VMEM((1, 4 * tq), F32),
                        pltpu.VMEM((tq, 4 * tq), F32), pltpu.VMEM((1, 4 * tq), F32)],
        compiler_params=pltpu.CompilerParams(dimension_semantics=("parallel", "arbitrary"),
                                             vmem_limit_bytes=VMEM_LIMIT),
        name="attn_prompt",
    )(lam_vecs, q, k, v, subln_w)


def _col_broadcast(row):
    return jnp.broadcast_to(row, (LANES, LANES)).T


def _attn_paged_kernel(pt_ref, lam_ref, q_ref, kn_ref, vn_ref, sw_ref, ck_hbm, cv_hbm, o_ref, kbuf, vbuf, sem,
                       qall_sc, sa_sc, smaxa_sc, sb_sc, smaxb_sc, m_sc, l_sc, acc_sc, *,
                       pages_per_group, n_groups, n_batch, t_new):
    b = pl.program_id(0)
    n_slots = kbuf.shape[0]
    lookahead = n_slots - 2
    n_pages = pages_per_group * n_groups
    rows_pg = PAGE_SIZE * N_KV_HEADS
    head_shift = int(math.log2(N_KV_HEADS))
    col_shift = int(math.log2(4 * t_new))
    row_i = lax.broadcasted_iota(jnp.int32, (rows_pg, LANES), 0)
    col_i = lax.broadcasted_iota(jnp.int32, (rows_pg, LANES), 1)
    head_match = (row_i & (N_KV_HEADS - 1)) == (col_i >> col_shift)

    def group_copies(batch, g):
        slot = g % n_slots
        copies = []
        for i in range(pages_per_group):
            page = pt_ref[batch * n_pages + g * pages_per_group + i]
            dst = pl.ds(i * rows_pg, rows_pg)
            copies.append(pltpu.make_async_copy(ck_hbm.at[page], kbuf.at[slot, dst], sem.at[0, slot]))
            copies.append(pltpu.make_async_copy(cv_hbm.at[page], vbuf.at[slot, dst], sem.at[1, slot]))
        return copies

    def start_group(batch, g):
        for cp in group_copies(batch, g):
            cp.start()

    def wait_group(batch, g):
        for cp in group_copies(batch, g):
            cp.wait()

    @pl.when(b == 0)
    def _():
        for g in range(lookahead):
            start_group(b, g)

    lane_lo = lax.broadcasted_iota(jnp.int32, (t_new, LANES), 1) < ATT_HEAD_DIM
    pieces = []
    for hr in range(2 * N_KV_HEADS):
        pair = q_ref[0, :, hr * HEAD_W:(hr + 1) * HEAD_W].astype(F32)
        zero = jnp.zeros_like(pair)
        pieces += [jnp.where(lane_lo, pair, zero), jnp.where(lane_lo, zero, pair)]
    qall_sc[...] = jnp.concatenate(pieces, axis=0).T.astype(BF16)
    m_sc[...] = jnp.full_like(m_sc, -jnp.inf)
    l_sc[...] = jnp.zeros_like(l_sc)
    acc_sc[...] = jnp.zeros_like(acc_sc)

    def stage_scores(stage, n_blocks, rows, score_fn):
        s_ref, smax_ref = stage
        smax = jnp.full((1, LANES), -jnp.inf, F32)
        for i in range(n_blocks):
            s = score_fn(i)
            s_ref[i * rows_pg:i * rows_pg + rows, :] = s
            smax = jnp.maximum(smax, jnp.max(s, axis=0, keepdims=True))
        smax_ref[...] = smax

    def consume(stage, n_blocks, rows, value_fn):
        s_ref, smax_ref = stage
        m_prev = m_sc[...]
        m_new = jnp.maximum(m_prev, smax_ref[...])
        alpha = jnp.exp2(m_prev - m_new)
        lsum = jnp.zeros_like(m_new)
        pv = jnp.zeros((LANES, HEAD_W), F32)
        for i in range(n_blocks):
            p = jnp.exp2(s_ref[i * rows_pg:i * rows_pg + rows, :] - m_new)
            lsum = lsum + jnp.sum(p, axis=0, keepdims=True)
            pv = pv + jnp.dot(p.T.astype(BF16), value_fn(i), preferred_element_type=F32)
        l_sc[...] = alpha * l_sc[...] + lsum
        acc_sc[...] = _col_broadcast(alpha) * acc_sc[...] + pv
        m_sc[...] = m_new

    stages = ((sa_sc, smaxa_sc), (sb_sc, smaxb_sc))

    def group_scores(g):
        slot = g % n_slots

        def page_scores(i):
            page = kbuf[slot, i * rows_pg:(i + 1) * rows_pg, :].astype(BF16)
            return jnp.where(head_match, jnp.dot(page, qall_sc[...], preferred_element_type=F32), -jnp.inf)

        stage_scores(stages[g % 2], pages_per_group, rows_pg, page_scores)

    def group_consume(g):
        slot = g % n_slots
        consume(stages[g % 2], pages_per_group, rows_pg,
                lambda i: vbuf[slot, i * rows_pg:(i + 1) * rows_pg, :].astype(BF16))

    for g in range(n_groups):
        wait_group(b, g)
        nxt = g + lookahead
        if nxt < n_groups:
            start_group(b, nxt)
        else:
            @pl.when(b + 1 < n_batch)
            def _():
                start_group(b + 1, nxt - n_groups)
        group_scores(g)
        if g >= 1:
            group_consume(g - 1)
    group_consume(n_groups - 1)

    pad = jnp.zeros((LANES - t_new * N_KV_HEADS, HEAD_W), F32)
    kn = jnp.concatenate([kn_ref[0], pad], axis=0).astype(BF16)
    vn = jnp.concatenate([vn_ref[0], pad], axis=0).astype(BF16)
    row_n = lax.broadcasted_iota(jnp.int32, (LANES, LANES), 0)
    col_n = lax.broadcasted_iota(jnp.int32, (LANES, LANES), 1)
    live = (((row_n & (N_KV_HEADS - 1)) == (col_n >> col_shift))
            & ((row_n >> head_shift) <= (col_n & (t_new - 1))))
    s_new = jnp.where(live, jnp.dot(kn, qall_sc[...], preferred_element_type=F32), -jnp.inf)
    stage_scores(stages[0], 1, LANES, lambda i: s_new)
    consume(stages[0], 1, LANES, lambda i: vn)
    lam = _lambda(lam_ref)
    sw = sw_ref[...]
    on = acc_sc[...] / _col_broadcast(l_sc[...])
    for hr in range(2 * N_KV_HEADS):
        base = hr * 2 * t_new
        o = on[base:base + t_new, :] - lam * on[base + t_new:base + 2 * t_new, :]
        ms = jnp.mean(o * o, axis=-1, keepdims=True)
        o = o * lax.rsqrt(ms + SUBLN_EPS) * sw * (1.0 - LAMBDA_INIT)
        o_ref[0, :, hr * HEAD_W:(hr + 1) * HEAD_W] = o.astype(o_ref.dtype)


def _attn_paged(q, k_new, v_new, cache_k, cache_v, page_table, lam_vecs, subln_w, pages_per_group=8, n_slots=4):
    b, t_new, _ = q.shape
    n_pages = page_table.shape[1]
    n_groups = n_pages // pages_per_group
    rows_pg = PAGE_SIZE * N_KV_HEADS
    assert N_KV_HEADS * 4 * t_new == LANES
    assert n_groups % n_slots == 0 and n_groups >= n_slots
    pt_flat = page_table.reshape(-1)

    per_b = lambda r, w: pl.BlockSpec((1, r, w), lambda bi, pt: (bi, 0, 0))
    const = lambda shape: pl.BlockSpec(shape, lambda bi, pt: (0,) * len(shape))
    hbm = pl.BlockSpec(memory_space=pl.ANY)
    group_rows = pages_per_group * rows_pg
    grid_spec = pltpu.PrefetchScalarGridSpec(
        num_scalar_prefetch=1,
        grid=(b,),
        in_specs=[const(lam_vecs.shape), per_b(t_new, ATT_WIDTH), per_b(t_new * N_KV_HEADS, HEAD_W),
                  per_b(t_new * N_KV_HEADS, HEAD_W), const((1, HEAD_W)), hbm, hbm],
        out_specs=per_b(t_new, ATT_WIDTH),
        scratch_shapes=[pltpu.VMEM((n_slots, group_rows, HEAD_W), F32), pltpu.VMEM((n_slots, group_rows, HEAD_W), F32),
                        pltpu.SemaphoreType.DMA((2, n_slots)),
                        pltpu.VMEM((HEAD_W, LANES), BF16),
                        pltpu.VMEM((group_rows, LANES), F32), pltpu.VMEM((1, LANES), F32),
                        pltpu.VMEM((group_rows, LANES), F32), pltpu.VMEM((1, LANES), F32),
                        pltpu.VMEM((1, LANES), F32), pltpu.VMEM((1, LANES), F32), pltpu.VMEM((LANES, HEAD_W), F32)],
    )
    return pl.pallas_call(
        functools.partial(_attn_paged_kernel, pages_per_group=pages_per_group, n_groups=n_groups, n_batch=b,
                          t_new=t_new),
        grid_spec=grid_spec,
        out_shape=jax.ShapeDtypeStruct((b, t_new, ATT_WIDTH), BF16),
        compiler_params=pltpu.CompilerParams(dimension_semantics=("arbitrary",),
                                             vmem_limit_bytes=VMEM_LIMIT),
        name="attn_paged",
    )(pt_flat, lam_vecs, q, k_new, v_new, subln_w, cache_k, cache_v)


def _mix_kernel(y_ref, z_ref, o_ref, g_ref, x_ref, nw_ref, wso_ref, wao_ref, wo_ref, out_ref):
    y = y_ref[...].astype(F32) * _silu_tanh(z_ref[...].astype(F32))
    gw = D_INNER // N_SSM_GROUPS
    parts = []
    for g in range(N_SSM_GROUPS):
        yg = y[:, g * gw:(g + 1) * gw]
        ms = jnp.mean(yg * yg, axis=-1, keepdims=True)
        parts.append((yg * lax.rsqrt(ms + NORM_EPS) * nw_ref[:, g * gw:(g + 1) * gw]).astype(BF16))
    y_ssm = jnp.dot(jnp.concatenate(parts, axis=1), wso_ref[...], preferred_element_type=F32)
    y_att = jnp.dot(o_ref[...], wao_ref[...], preferred_element_type=F32)
    gates = 0.5 * jnp.tanh(0.5 * g_ref[...].astype(F32)) + 0.5
    mixed = (gates[:, :D_MODEL] * y_ssm + gates[:, D_MODEL:] * y_att).astype(BF16)
    out_ref[...] = x_ref[...] + jnp.dot(mixed, wo_ref[...], preferred_element_type=F32)


def _mix(y, z, o, gates, x, ssm_norm_w, w_ssm_out, w_att_out, w_o, tm=512):
    n = x.shape[0]
    row = lambda w: pl.BlockSpec((tm, w), lambda i: (i, 0))
    return pl.pallas_call(
        _mix_kernel,
        grid=(n // tm,),
        in_specs=[row(D_INNER), row(D_INNER), row(ATT_WIDTH), row(2 * D_MODEL), row(D_MODEL),
                  _resident((1, D_INNER)), _resident((D_INNER, D_MODEL)), _resident((ATT_WIDTH, D_MODEL)),
                  _resident((D_MODEL, D_MODEL))],
        out_specs=row(D_MODEL),
        out_shape=jax.ShapeDtypeStruct((n, D_MODEL), F32),
        compiler_params=pltpu.CompilerParams(dimension_semantics=("parallel",), vmem_limit_bytes=VMEM_LIMIT),
        name="mix",
    )(y, z, o, gates, x, ssm_norm_w, w_ssm_out, w_att_out, w_o)


def _ffn_kernel(x_ref, nw_ref, win_ref, wout_ref, fw_ref, out_ref, *, chunk):
    x = x_ref[...]
    ms = jnp.mean(x * x, axis=-1, keepdims=True)
    h = (x * lax.rsqrt(ms + NORM_EPS) * nw_ref[...]).astype(BF16)
    acc = x
    for c in range(0, D_FF, chunk):
        gate = jnp.dot(h, win_ref[:, c:c + chunk], preferred_element_type=F32)
        up = jnp.dot(h, win_ref[:, D_FF + c:D_FF + c + chunk], preferred_element_type=F32)
        hid = (_silu_tanh(gate) * up).astype(BF16)
        acc = acc + jnp.dot(hid, wout_ref[c:c + chunk, :], preferred_element_type=F32)
    ms2 = jnp.mean(acc * acc, axis=-1, keepdims=True)
    out_ref[...] = acc * lax.rsqrt(ms2 + NORM_EPS) * fw_ref[...]


def _ffn(x, norm_w, w_in, w_out, final_w, tm=1024, chunk=256):
    n = x.shape[0]
    tm = min(tm, n)
    row = pl.BlockSpec((tm, D_MODEL), lambda i: (i, 0))
    return pl.pallas_call(
        functools.partial(_ffn_kernel, chunk=chunk),
        grid=(n // tm,),
        in_specs=[row, _resident((1, D_MODEL)), _resident((D_MODEL, 2 * D_FF)), _resident((D_FF, D_MODEL)),
                  _resident((1, D_MODEL))],
        out_specs=row,
        out_shape=jax.ShapeDtypeStruct((n, D_MODEL), F32),
        compiler_params=pltpu.CompilerParams(dimension_semantics=("parallel",), vmem_limit_bytes=VMEM_LIMIT),
        name="ffn",
    )(x, norm_w, w_in, w_out, final_w)


def _pack_w_in(w_in):
    cuts = np.cumsum((D_INNER, CONV_DIM, N_SSM_HEADS, ATT_WIDTH, KV_WIDTH, KV_WIDTH, 2 * D_MODEL))[:-1]
    wz, wxbc, wdt, wq, wk, wv, wg = jnp.split(w_in, [int(c) for c in cuts], axis=-1)
    wdt = jnp.pad(wdt, ((0, 0), (0, DT_PAD - N_SSM_HEADS)))
    wq = wq * (ATT_HEAD_DIM ** -0.5 * math.log2(math.e))
    return jnp.concatenate([wz, wxbc, wq, wk, wv, wg, wdt], axis=-1).astype(BF16)


def kernel(x_prompt, x_sample, cache_k, cache_v, state_conv, state_ssm, page_table, norm_mix_w, w_in, conv_w, conv_b, dt_bias, a_log, d_skip, ssm_norm_w, w_ssm_out, lambda_q1, lambda_k1, lambda_q2, lambda_k2, subln_w, w_att_out, w_o, norm_ffn_w, w_ffn_in, w_ffn_out, norm_final_w):
    l = 0
    w_all = _pack_w_in(w_in[l])
    pad_heads = lambda t: jnp.pad(t.reshape(1, N_SSM_HEADS), ((0, 0), (0, DT_PAD - N_SSM_HEADS)))
    lam_vecs = jnp.stack([lambda_q1[l], lambda_k1[l], lambda_q2[l], lambda_k2[l]])
    params = dict(
        gain=norm_mix_w[l].reshape(1, D_MODEL), conv_w=conv_w[l], conv_b=conv_b[l].reshape(1, CONV_DIM),
        dt_bias=pad_heads(dt_bias[l]), a_log=pad_heads(a_log[l]),
        d_skip_x=jnp.repeat(d_skip[l], SSM_HEAD_DIM).reshape(1, D_INNER),
        ssm_norm_w=ssm_norm_w[l].reshape(1, D_INNER), w_ssm_out=w_ssm_out[l].astype(BF16),
        subln_w=subln_w[l].reshape(1, 2 * ATT_HEAD_DIM), w_att_out=w_att_out[l].astype(BF16),
        w_o=w_o[l].astype(BF16), norm_ffn_w=norm_ffn_w[l].reshape(1, D_MODEL),
        w_ffn_in=w_ffn_in[l].astype(BF16), w_ffn_out=w_ffn_out[l].astype(BF16),
        norm_final_w=norm_final_w.reshape(1, D_MODEL))

    def group(x, conv0, h0, attend):
        b, seq, _ = x.shape
        n = b * seq
        x2 = x.reshape(n, D_MODEL)
        z, xbc, q, k, v, gates, dt = _inproj(x2, params["gain"], w_all)
        y, conv_state, ssm_state = _ssd(
            xbc.reshape(b, seq, CONV_DIM), dt.reshape(b, seq, DT_PAD), conv0, h0.reshape(b, D_INNER, D_STATE),
            params["conv_w"], params["conv_b"], params["dt_bias"], params["a_log"], params["d_skip_x"])
        k3 = k.reshape(b, seq * N_KV_HEADS, HEAD_W)
        v3 = v.reshape(b, seq * N_KV_HEADS, HEAD_W)
        o = attend(q.reshape(b, seq, ATT_WIDTH), k3, v3)
        x1 = _mix(y.reshape(n, D_INNER), z, o.reshape(n, ATT_WIDTH), gates, x2, params["ssm_norm_w"],
                  params["w_ssm_out"], params["w_att_out"], params["w_o"])
        out = _ffn(x1, params["norm_ffn_w"], params["w_ffn_in"], params["w_ffn_out"], params["norm_final_w"])
        kv_shape = (1, b, seq, N_KV_HEADS, 2 * ATT_HEAD_DIM)
        return (out.reshape(b, seq, D_MODEL), k.reshape(kv_shape), v.reshape(kv_shape), conv_state[None],
                ssm_state.reshape(1, b, N_SSM_HEADS, SSM_HEAD_DIM, D_STATE))

    bp = x_prompt.shape[0]
    attend_p = lambda q, k, v: _attn_prompt(q, k, v, lam_vecs, params["subln_w"])
    yp, kp, vp, cp, hp = group(x_prompt, jnp.zeros((bp, D_CONV - 1, CONV_DIM), F32),
                               jnp.zeros((bp, N_SSM_HEADS, SSM_HEAD_DIM, D_STATE), F32), attend_p)
    n_pool = cache_k.shape[1]
    ck = cache_k[l].reshape(n_pool, PAGE_SIZE * N_KV_HEADS, HEAD_W)
    cv = cache_v[l].reshape(n_pool, PAGE_SIZE * N_KV_HEADS, HEAD_W)
    attend_s = lambda q, k, v: _attn_paged(q, k, v, ck, cv, page_table, lam_vecs, params["subln_w"])
    ys, ks, vs, cs, hs = group(x_sample, state_conv[l], state_ssm[l], attend_s)
    return (yp, ys, kp, vp, cp, hp, ks, vs, cs, hs)
```

```python
import functools
import math

import jax
import jax.numpy as jnp
import numpy as np
from jax import lax
from jax.experimental import pallas as pl
from jax.experimental.pallas import tpu as pltpu

F32 = jnp.float32
BF16 = jnp.bfloat16

D_MODEL = 1024
D_INNER = 2048
SSM_HEAD_DIM = 64
N_SSM_HEADS = 32
N_SSM_GROUPS = 4
D_STATE = 128
D_CONV = 4
CONV_DIM = 3072
ATT_HEAD_DIM = 64
N_KV_HEADS = 4
ATT_WIDTH = 1024
KV_WIDTH = 512
D_FF = 2816
PAGE_SIZE = 128
HEAD_W = 2 * ATT_HEAD_DIM
NORM_EPS = 1e-6
SUBLN_EPS = 1e-5
LAMBDA_INIT = 0.8 - 0.6 * math.exp(-0.3 * 0)

LANES = 128
bf16_tile_rows = 16
CHUNK = 128
DT_PAD = LANES
VMEM_LIMIT = 56 * 1024 * 1024

SEG_Z, SEG_XBC, SEG_Q, SEG_K, SEG_V, SEG_G, SEG_DT = 0, 2048, 5120, 6144, 6656, 7168, 9216
W_ALL = SEG_DT + DT_PAD


def _resident(shape):
    nd = len(shape)
    return pl.BlockSpec(shape, lambda *_: (0,) * nd, pipeline_mode=pl.Buffered(1))


def _silu_tanh(x):
    h = 0.5 * x
    return h + h * jnp.tanh(h)


def _inproj_kernel(x_ref, g_ref, w_ref, *out_refs, segs):
    x = x_ref[...]
    tm = x_ref.shape[0]
    ms = jnp.mean(x * x, axis=-1, keepdims=True)
    h = (x * lax.rsqrt(ms + NORM_EPS) * g_ref[...]).astype(BF16)
    for ref, (start, width, _, kv_rows) in zip(out_refs, segs):
        step = min(width, 512)
        for c in range(0, width, step):
            r = jnp.dot(h, w_ref[:, start + c:start + c + step], preferred_element_type=F32)
            if kv_rows:
                for hd in range(c // HEAD_W, (c + step) // HEAD_W):
                    ref[pl.ds(hd, tm, stride=N_KV_HEADS), :] = r[:, hd * HEAD_W - c:(hd + 1) * HEAD_W - c]
            else:
                ref[:, c:c + step] = r.astype(ref.dtype)


def _inproj_call(x2d, gain, w, segs, tm):
    n = x2d.shape[0]
    specs, shapes = [], []
    for _, width, dtype, kv_rows in segs:
        if kv_rows:
            specs.append(pl.BlockSpec((tm * N_KV_HEADS, HEAD_W), lambda i: (i, 0)))
            shapes.append(jax.ShapeDtypeStruct((n * N_KV_HEADS, HEAD_W), dtype))
        else:
            specs.append(pl.BlockSpec((tm, width), lambda i: (i, 0)))
            shapes.append(jax.ShapeDtypeStruct((n, width), dtype))
    return pl.pallas_call(
        functools.partial(_inproj_kernel, segs=segs),
        grid=(n // tm,),
        in_specs=[pl.BlockSpec((tm, D_MODEL), lambda i: (i, 0)), _resident((1, D_MODEL)), _resident(w.shape)],
        out_specs=specs,
        out_shape=shapes,
        compiler_params=pltpu.CompilerParams(dimension_semantics=("parallel",), vmem_limit_bytes=VMEM_LIMIT),
        name="in_proj",
    )(x2d, gain, w)


def _inproj(x2d, gain, w_all, tm=512):
    segs = ((SEG_Z, D_INNER, BF16, False), (SEG_XBC, CONV_DIM, BF16, False), (SEG_Q, ATT_WIDTH, BF16, False),
            (SEG_K, KV_WIDTH, F32, True), (SEG_V, KV_WIDTH, F32, True), (SEG_G, 2 * D_MODEL, BF16, False),
            (SEG_DT, DT_PAD, F32, False))
    return _inproj_call(x2d, gain, w_all, segs, tm)


def _split3_packed(v):
    a = v.astype(BF16).astype(F32)
    r = v - a
    b = r.astype(BF16).astype(F32)
    c = (r - b).astype(BF16).astype(F32)
    packed = a + pltpu.roll(b, N_SSM_HEADS, axis=1) + pltpu.roll(c, 2 * N_SSM_HEADS, axis=1)
    return packed.astype(BF16)


def _ssd_kernel(xbc_ref, dt_ref, pre_ref, st0_ref, cw_ref, cb_ref, dtb_ref, alog_ref, dsk_ref, e64_ref, e128_ref,
                y_ref, conv_ref, st_ref, xp_sc, act_sc, st_sc, cst_sc, *, t_in, n_chunks):
    c = pl.program_id(1)
    T = CHUNK
    row_pad = 8
    live = T

    @pl.when(c == 0)
    def _():
        xp_sc[...] = jnp.zeros_like(xp_sc)
        xp_sc[row_pad - (D_CONV - 1):row_pad, :] = pre_ref[0]
        for blk in range(D_INNER // LANES):
            st_sc[:, blk * LANES:(blk + 1) * LANES] = st0_ref[0, blk * LANES:(blk + 1) * LANES, :].T

    xp_sc[row_pad:row_pad + t_in, :] = xbc_ref[0].astype(F32)

    cblk = 256
    for j in range(CONV_DIM // cblk):
        cs_ = slice(j * cblk, (j + 1) * cblk)
        window = xp_sc[0:row_pad + live, cs_]
        acc = cb_ref[:, cs_] + window[row_pad:] * cw_ref[D_CONV - 1:D_CONV, cs_]
        for k in range(D_CONV - 1):
            shifted = pltpu.roll(window, D_CONV - 1 - k, axis=0)[row_pad:]
            acc = acc + shifted * cw_ref[k:k + 1, cs_]
        act_sc[0:live, cs_] = _silu_tanh(acc)

    lane = lax.broadcasted_iota(jnp.int32, (T, LANES), 1)
    rowi = lax.broadcasted_iota(jnp.int32, (T, LANES), 0)
    if t_in < T:
        dt_raw = jnp.concatenate([dt_ref[0], jnp.zeros((T - t_in, DT_PAD), F32)], axis=0)
    else:
        dt_raw = dt_ref[0]
    v = dt_raw + dtb_ref[...]
    softplus = jnp.maximum(v, 0.0) + jnp.log1p(jnp.exp(-jnp.abs(v)))
    dt = jnp.where((lane < N_SSM_HEADS) & (rowi < t_in), softplus, 0.0)
    la = dt * (-jnp.exp(alog_ref[...]))
    cs = la
    d = 1
    while d < T:
        cs = cs + jnp.where(rowi >= d, pltpu.roll(cs, d, axis=0), 0.0)
        d *= 2
    cs = cs * math.log2(math.e)
    cst_sc[...] = cs.T
    x_dt = _split3_packed(dt)
    x_cs = _split3_packed(cs)
    x_cs_live = _split3_packed(cs[0:live])
    causal = (lax.broadcasted_iota(jnp.int32, (live, LANES), 0) >= lax.broadcasted_iota(jnp.int32, (live, LANES), 1))
    lo_half = lane < SSM_HEAD_DIM

    heads_per_group = N_SSM_HEADS // N_SSM_GROUPS
    gw = heads_per_group * SSM_HEAD_DIM
    for g in range(N_SSM_GROUPS):
        c_col = D_INNER + N_SSM_GROUPS * D_STATE + g * D_STATE
        bm = act_sc[:, D_INNER + g * D_STATE:D_INNER + (g + 1) * D_STATE]
        bm_b = bm.astype(BF16)
        cm_b = act_sc[0:live, c_col:c_col + D_STATE].astype(BF16)
        bmt_b = bm.T.astype(BF16)
        cb = lax.dot_general(cm_b, bm_b, (((1,), (1,)), ((), ())), preferred_element_type=F32)
        gs = slice(g * gw, (g + 1) * gw)
        dtx = jnp.dot(x_dt, e64_ref[:, gs], preferred_element_type=F32)
        csx = jnp.dot(x_cs, e64_ref[:, gs], preferred_element_type=F32)
        cs_last = csx[T - 1:T, :]
        xs = act_sc[:, gs]
        xd = xs * dtx
        xdd_b = (xd * jnp.exp2(cs_last - csx)).astype(BF16)
        xd_b = xd.astype(BF16)
        st_old = st_sc[:, gs]
        y_off = jnp.dot(cm_b, st_old.astype(BF16), preferred_element_type=F32) * jnp.exp2(csx[0:live])
        st_sc[:, gs] = st_old * jnp.exp2(cs_last) + jnp.dot(bmt_b, xdd_b, preferred_element_type=F32)
        y_skip = xs[0:live] * dsk_ref[:, gs]
        for pr in range(heads_per_group // 2):
            ms = []
            for hh in range(2):
                h = g * heads_per_group + pr * 2 + hh
                cs_col = jnp.dot(x_cs_live, e128_ref[:, h * LANES:(h + 1) * LANES], preferred_element_type=F32)
                seg = jnp.where(causal, cs_col - cst_sc[h:h + 1, :], -jnp.inf)
                ms.append((cb * jnp.exp2(seg)).astype(BF16))
            ps = slice(pr * LANES, (pr + 1) * LANES)
            xd_p = xd_b[:, ps]
            zero = jnp.zeros_like(xd_p)
            rhs = jnp.concatenate([jnp.where(lo_half, xd_p, zero), jnp.where(lo_half, zero, xd_p)], axis=0)
            y_diag = jnp.dot(jnp.concatenate(ms, axis=1), rhs, preferred_element_type=F32)
            y = y_diag + y_off[:, ps] + y_skip[:, ps]
            y_ref[0, :, g * gw + pr * LANES:g * gw + (pr + 1) * LANES] = y[:t_in].astype(y_ref.dtype)

    @pl.when(c == n_chunks - 1)
    def _():
        conv_ref[0] = xp_sc[row_pad + t_in - (D_CONV - 1):row_pad + t_in, :]
        for blk in range(D_INNER // LANES):
            st_ref[0, blk * LANES:(blk + 1) * LANES, :] = st_sc[:, blk * LANES:(blk + 1) * LANES].T

    if n_chunks > 1:
        xp_sc[0:row_pad, :] = xp_sc[T:T + row_pad, :]


def _expansion_matrices():
    e64 = np.zeros((LANES, D_INNER), np.float32)
    e128 = np.zeros((LANES, N_SSM_HEADS * LANES), np.float32)
    for part in range(3):
        for h in range(N_SSM_HEADS):
            e64[part * N_SSM_HEADS + h, h * SSM_HEAD_DIM:(h + 1) * SSM_HEAD_DIM] = 1.0
            e128[part * N_SSM_HEADS + h, h * LANES:(h + 1) * LANES] = 1.0
    return jnp.asarray(e64, BF16), jnp.asarray(e128, BF16)


def _ssd(xbc, dt, prefix, st0, conv_w, conv_b, dt_bias, a_log, d_skip_x):
    b, l, _ = xbc.shape
    t_in = min(l, CHUNK)
    n_chunks = l // t_in
    e64, e128 = _expansion_matrices()
    per_b = lambda shape: pl.BlockSpec((1,) + shape, lambda i, c: (i, 0, 0))
    per_chunk = lambda w: pl.BlockSpec((1, t_in, w), lambda i, c: (i, c, 0))
    return pl.pallas_call(
        functools.partial(_ssd_kernel, t_in=t_in, n_chunks=n_chunks),
        grid=(b, n_chunks),
        in_specs=[per_chunk(CONV_DIM), per_chunk(DT_PAD), per_b((D_CONV - 1, CONV_DIM)), per_b((D_INNER, D_STATE)),
                  _resident((D_CONV, CONV_DIM)), _resident((1, CONV_DIM)), _resident((1, DT_PAD)),
                  _resident((1, DT_PAD)), _resident((1, D_INNER)), _resident(e64.shape), _resident(e128.shape)],
        out_specs=[per_chunk(D_INNER), per_b((D_CONV - 1, CONV_DIM)), per_b((D_INNER, D_STATE))],
        out_shape=[jax.ShapeDtypeStruct((b, l, D_INNER), BF16),
                   jax.ShapeDtypeStruct((b, D_CONV - 1, CONV_DIM), F32),
                   jax.ShapeDtypeStruct((b, D_INNER, D_STATE), F32)],
        scratch_shapes=[pltpu.VMEM((8 + CHUNK, CONV_DIM), F32), pltpu.VMEM((CHUNK, CONV_DIM), F32),
                        pltpu.VMEM((D_STATE, D_INNER), F32), pltpu.VMEM((LANES, CHUNK), F32)],
        compiler_params=pltpu.CompilerParams(dimension_semantics=("parallel", "arbitrary"),
                                             vmem_limit_bytes=VMEM_LIMIT),
        name="ssd",
    )(xbc, dt, prefix, st0, conv_w, conv_b, dt_bias, a_log, d_skip_x, e64, e128)


def _lambda(lam_ref):
    lv = lam_ref[...]
    s1 = jnp.sum(lv[0:1] * lv[1:2], axis=-1, keepdims=True)
    s2 = jnp.sum(lv[2:3] * lv[3:4], axis=-1, keepdims=True)
    return jnp.exp(s1) - jnp.exp(s2) + LAMBDA_INIT


def _attn_prompt_kernel(lam_ref, q_ref, k_ref, v_ref, sw_ref, o_ref, kall_sc, vtall_sc, qt_sc, m_sc, acc_sc,
                        sa_sc, smaxa_sc, sb_sc, smaxb_sc, *, tq, seq):
    h = pl.program_id(1)
    extra = vtall_sc.shape[1] - HEAD_W

    @pl.when(h == 0)
    def _():
        ones_row = (lax.broadcasted_iota(jnp.int32, (extra, seq), 0) == 0).astype(F32).astype(BF16)
        for hd in range(N_KV_HEADS):
            kall_sc[hd] = k_ref[pl.ds(hd, seq, stride=N_KV_HEADS), :].astype(BF16)
            for blk in range(seq // LANES):
                vb = v_ref[pl.ds(hd + blk * LANES * N_KV_HEADS, LANES, stride=N_KV_HEADS), :]
                vtall_sc[hd, 0:HEAD_W, blk * LANES:(blk + 1) * LANES] = vb.T.astype(BF16)
            vtall_sc[hd, HEAD_W:, :] = ones_row

    k_sc = kall_sc.at[h]
    vt_sc = vtall_sc.at[h]
    lam = _lambda(lam_ref)
    sw = sw_ref[...]
    feat_lo = lax.broadcasted_iota(jnp.int32, (LANES, tq), 0) < ATT_HEAD_DIM

    def kv_slice(start):
        return pl.ds(pl.multiple_of(start, tq), tq)

    def scores(start, stage, causal=None):
        s_ref, smax_ref = stage
        s = jnp.dot(k_sc[kv_slice(start), :], qt_sc[...], preferred_element_type=F32)
        if causal is not None:
            s = jnp.where(causal, s, -jnp.inf)
        s_ref[...] = s
        smax_ref[...] = jnp.max(s, axis=0, keepdims=True)

    def consume(start, stage):
        s_ref, smax_ref = stage
        m_prev = m_sc[...]
        m_new = jnp.maximum(m_prev, smax_ref[...])
        alpha = jnp.exp2(m_prev - m_new)
        p = jnp.exp2(s_ref[...] - m_new)
        pv = jnp.dot(vt_sc[:, kv_slice(start)], p.astype(BF16), preferred_element_type=F32)
        acc_sc[...] = alpha * acc_sc[...] + pv
        m_sc[...] = m_new

    stage_a, stage_b = (sa_sc, smaxa_sc), (sb_sc, smaxb_sc)

    def q_block(qi, carry):
        qs = pl.ds(pl.multiple_of(qi * tq, tq), tq)
        for r in range(2):
            pair_t = q_ref[0, qs, r * LANES:(r + 1) * LANES].astype(F32).T
            zero = jnp.zeros_like(pair_t)
            qt_sc[:, (2 * r) * tq:(2 * r + 1) * tq] = jnp.where(feat_lo, pair_t, zero).astype(BF16)
            qt_sc[:, (2 * r + 1) * tq:(2 * r + 2) * tq] = jnp.where(feat_lo, zero, pair_t).astype(BF16)
        m_sc[...] = jnp.full_like(m_sc, -jnp.inf)
        acc_sc[...] = jnp.zeros_like(acc_sc)

        key_i = lax.broadcasted_iota(jnp.int32, (tq, 4 * tq), 0)
        qt_i = lax.broadcasted_iota(jnp.int32, (tq, 4 * tq), 1) & (tq - 1)
        scores(qi * tq, stage_a, causal=key_i <= qt_i)

        def kv_pair(t, start_in_a):
            scores(2 * t * tq, stage_b)
            consume(start_in_a, stage_a)
            scores((2 * t + 1) * tq, stage_a)
            consume(2 * t * tq, stage_b)
            return (2 * t + 1) * tq

        start_in_a = lax.fori_loop(0, qi // 2, kv_pair, qi * tq)

        @pl.when(qi % 2 == 1)
        def _():
            scores((qi - 1) * tq, stage_b)
            consume(start_in_a, stage_a)
            consume((qi - 1) * tq, stage_b)

        @pl.when(qi % 2 == 0)
        def _():
            consume(start_in_a, stage_a)

        on = acc_sc[0:HEAD_W, :] / acc_sc[HEAD_W:HEAD_W + 1, :]
        for r in range(2):
            o_t = on[:, (2 * r) * tq:(2 * r + 1) * tq] - lam * on[:, (2 * r + 1) * tq:(2 * r + 2) * tq]
            o = o_t.T
            ms = jnp.mean(o * o, axis=-1, keepdims=True)
            o = o * lax.rsqrt(ms + SUBLN_EPS) * sw * (1.0 - LAMBDA_INIT)
            o_ref[0, qs, r * LANES:(r + 1) * LANES] = o.astype(o_ref.dtype)
        return carry

    lax.fori_loop(0, seq // tq, q_block, 0)


def _attn_prompt(q, k, v, lam_vecs, subln_w, tq=256):
    b, seq, _ = q.shape
    kv_spec = pl.BlockSpec((None, seq * N_KV_HEADS, HEAD_W), lambda i, h: (i, 0, 0))
    return pl.pallas_call(
        functools.partial(_attn_prompt_kernel, tq=tq, seq=seq),
        grid=(b, N_KV_HEADS),
        in_specs=[_resident(lam_vecs.shape),
                  pl.BlockSpec((1, seq, 2 * HEAD_W), lambda i, h: (i, 0, h)),
                  kv_spec, kv_spec, _resident((1, HEAD_W))],
        out_specs=pl.BlockSpec((1, seq, 2 * HEAD_W), lambda i, h: (i, 0, h)),
        out_shape=jax.ShapeDtypeStruct((b, seq, ATT_WIDTH), BF16),
        scratch_shapes=[pltpu.VMEM((N_KV_HEADS, seq, HEAD_W), BF16),
                        pltpu.VMEM((N_KV_HEADS, HEAD_W + bf16_tile_rows, seq), BF16),
                        pltpu.VMEM((HEAD_W, 4 * tq), BF16), pltpu.VMEM((1, 4 * tq), F32),
                        pltpu.VMEM((HEAD_W + bf16_tile_rows, 4 * tq), F32),
                        pltpu.VMEM((tq, 4 * tq), F32), pltpu.VMEM((1, 4 * tq), F32),
                        pltpu.VMEM((tq, 4 * tq), F32), pltpu.VMEM((1, 4 * tq), F32)],
        compiler_params=pltpu.CompilerParams(dimension_semantics=("parallel", "arbitrary"),
                                             vmem_limit_bytes=VMEM_LIMIT),
        name="attn_prompt",
    )(lam_vecs, q, k, v, subln_w)


def _col_broadcast(row):
    return jnp.broadcast_to(row, (LANES, LANES)).T


def _attn_paged_kernel(pt_ref, lam_ref, q_ref, kn_ref, vn_ref, sw_ref, ck_hbm, cv_hbm, o_ref, kbuf, vbuf, sem,
                       qall_sc, sa_sc, smaxa_sc, sb_sc, smaxb_sc, m_sc, l_sc, acc_sc, *,
                       pages_per_group, n_groups, n_batch, t_new):
    b = pl.program_id(0)
    n_slots = kbuf.shape[0]
    lookahead = n_slots - 2
    n_pages = pages_per_group * n_groups
    rows_pg = PAGE_SIZE * N_KV_HEADS
    head_shift = int(math.log2(N_KV_HEADS))
    col_shift = int(math.log2(4 * t_new))
    row_i = lax.broadcasted_iota(jnp.int32, (rows_pg, LANES), 0)
    col_i = lax.broadcasted_iota(jnp.int32, (rows_pg, LANES), 1)
    head_match = (row_i & (N_KV_HEADS - 1)) == (col_i >> col_shift)

    def group_copies(batch, g):
        slot = g % n_slots
        copies = []
        for i in range(pages_per_group):
            page = pt_ref[batch * n_pages + g * pages_per_group + i]
            dst = pl.ds(i * rows_pg, rows_pg)
            copies.append(pltpu.make_async_copy(ck_hbm.at[page], kbuf.at[slot, dst], sem.at[0, slot]))
            copies.append(pltpu.make_async_copy(cv_hbm.at[page], vbuf.at[slot, dst], sem.at[1, slot]))
        return copies

    def start_group(batch, g):
        for cp in group_copies(batch, g):
            cp.start()

    def wait_group(batch, g):
        for cp in group_copies(batch, g):
            cp.wait()

    @pl.when(b == 0)
    def _():
        for g in range(lookahead):
            start_group(b, g)

    lane_lo = lax.broadcasted_iota(jnp.int32, (t_new, LANES), 1) < ATT_HEAD_DIM
    pieces = []
    for hr in range(2 * N_KV_HEADS):
        pair = q_ref[0, :, hr * HEAD_W:(hr + 1) * HEAD_W].astype(F32)
        zero = jnp.zeros_like(pair)
        pieces += [jnp.where(lane_lo, pair, zero), jnp.where(lane_lo, zero, pair)]
    qall_sc[...] = jnp.concatenate(pieces, axis=0).T.astype(BF16)
    m_sc[...] = jnp.full_like(m_sc, -jnp.inf)
    l_sc[...] = jnp.zeros_like(l_sc)
    acc_sc[...] = jnp.zeros_like(acc_sc)

    def stage_scores(stage, n_blocks, rows, score_fn):
        s_ref, smax_ref = stage
        smax = jnp.full((1, LANES), -jnp.inf, F32)
        for i in range(n_blocks):
            s = score_fn(i)
            s_ref[i * rows_pg:i * rows_pg + rows, :] = s
            smax = jnp.maximum(smax, jnp.max(s, axis=0, keepdims=True))
        smax_ref[...] = smax

    def consume(stage, n_blocks, rows, value_fn):
        s_ref, smax_ref = stage
        m_prev = m_sc[...]
        m_new = jnp.maximum(m_prev, smax_ref[...])
        alpha = jnp.exp2(m_prev - m_new)
        lsum = jnp.zeros_like(m_new)
        pv = jnp.zeros((LANES, HEAD_W), F32)
        for i in range(n_blocks):
            p = jnp.exp2(s_ref[i * rows_pg:i * rows_pg + rows, :] - m_new)
            lsum = lsum + jnp.sum(p, axis=0, keepdims=True)
            pv = pv + jnp.dot(p.T.astype(BF16), value_fn(i), preferred_element_type=F32)
        l_sc[...] = alpha * l_sc[...] + lsum
        acc_sc[...] = _col_broadcast(alpha) * acc_sc[...] + pv
        m_sc[...] = m_new

    stages = ((sa_sc, smaxa_sc), (sb_sc, smaxb_sc))

    def group_scores(g):
        slot = g % n_slots

        def page_scores(i):
            page = kbuf[slot, i * rows_pg:(i + 1) * rows_pg, :].astype(BF16)
            return jnp.where(head_match, jnp.dot(page, qall_sc[...], preferred_element_type=F32), -jnp.inf)

        stage_scores(stages[g % 2], pages_per_group, rows_pg, page_scores)

    def group_consume(g):
        slot = g % n_slots
        consume(stages[g % 2], pages_per_group, rows_pg,
                lambda i: vbuf[slot, i * rows_pg:(i + 1) * rows_pg, :].astype(BF16))

    for g in range(n_groups):
        wait_group(b, g)
        nxt = g + lookahead
        if nxt < n_groups:
            start_group(b, nxt)
        else:
            @pl.when(b + 1 < n_batch)
            def _():
                start_group(b + 1, nxt - n_groups)
        group_scores(g)
        if g >= 1:
            group_consume(g - 1)
    group_consume(n_groups - 1)

    pad = jnp.zeros((LANES - t_new * N_KV_HEADS, HEAD_W), F32)
    kn = jnp.concatenate([kn_ref[0], pad], axis=0).astype(BF16)
    vn = jnp.concatenate([vn_ref[0], pad], axis=0).astype(BF16)
    row_n = lax.broadcasted_iota(jnp.int32, (LANES, LANES), 0)
    col_n = lax.broadcasted_iota(jnp.int32, (LANES, LANES), 1)
    live = (((row_n & (N_KV_HEADS - 1)) == (col_n >> col_shift))
            & ((row_n >> head_shift) <= (col_n & (t_new - 1))))
    s_new = jnp.where(live, jnp.dot(kn, qall_sc[...], preferred_element_type=F32), -jnp.inf)
    stage_scores(stages[0], 1, LANES, lambda i: s_new)
    consume(stages[0], 1, LANES, lambda i: vn)
    lam = _lambda(lam_ref)
    sw = sw_ref[...]
    on = acc_sc[...] / _col_broadcast(l_sc[...])
    for hr in range(2 * N_KV_HEADS):
        base = hr * 2 * t_new
        o = on[base:base + t_new, :] - lam * on[base + t_new:base + 2 * t_new, :]
        ms = jnp.mean(o * o, axis=-1, keepdims=True)
        o = o * lax.rsqrt(ms + SUBLN_EPS) * sw * (1.0 - LAMBDA_INIT)
        o_ref[0, :, hr * HEAD_W:(hr + 1) * HEAD_W] = o.astype(o_ref.dtype)


def _attn_paged(q, k_new, v_new, cache_k, cache_v, page_table, lam_vecs, subln_w, pages_per_group=8, n_slots=4):
    b, t_new, _ = q.shape
    n_pages = page_table.shape[1]
    n_groups = n_pages // pages_per_group
    rows_pg = PAGE_SIZE * N_KV_HEADS
    assert N_KV_HEADS * 4 * t_new == LANES
    assert n_groups % n_slots == 0 and n_groups >= n_slots
    pt_flat = page_table.reshape(-1)

    per_b = lambda r, w: pl.BlockSpec((1, r, w), lambda bi, pt: (bi, 0, 0))
    const = lambda shape: pl.BlockSpec(shape, lambda bi, pt: (0,) * len(shape))
    hbm = pl.BlockSpec(memory_space=pl.ANY)
    group_rows = pages_per_group * rows_pg
    grid_spec = pltpu.PrefetchScalarGridSpec(
        num_scalar_prefetch=1,
        grid=(b,),
        in_specs=[const(lam_vecs.shape), per_b(t_new, ATT_WIDTH), per_b(t_new * N_KV_HEADS, HEAD_W),
                  per_b(t_new * N_KV_HEADS, HEAD_W), const((1, HEAD_W)), hbm, hbm],
        out_specs=per_b(t_new, ATT_WIDTH),
        scratch_shapes=[pltpu.VMEM((n_slots, group_rows, HEAD_W), F32), pltpu.VMEM((n_slots, group_rows, HEAD_W), F32),
                        pltpu.SemaphoreType.DMA((2, n_slots)),
                        pltpu.VMEM((HEAD_W, LANES), BF16),
                        pltpu.VMEM((group_rows, LANES), F32), pltpu.VMEM((1, LANES), F32),
                        pltpu.VMEM((group_rows, LANES), F32), pltpu.VMEM((1, LANES), F32),
                        pltpu.VMEM((1, LANES), F32), pltpu.VMEM((1, LANES), F32), pltpu.VMEM((LANES, HEAD_W), F32)],
    )
    return pl.pallas_call(
        functools.partial(_attn_paged_kernel, pages_per_group=pages_per_group, n_groups=n_groups, n_batch=b,
                          t_new=t_new),
        grid_spec=grid_spec,
        out_shape=jax.ShapeDtypeStruct((b, t_new, ATT_WIDTH), BF16),
        compiler_params=pltpu.CompilerParams(dimension_semantics=("arbitrary",),
                                             vmem_limit_bytes=VMEM_LIMIT),
        name="attn_paged",
    )(pt_flat, lam_vecs, q, k_new, v_new, subln_w, cache_k, cache_v)


def _mix_kernel(y_ref, z_ref, o_ref, g_ref, x_ref, nw_ref, wso_ref, wao_ref, wo_ref, out_ref):
    y = y_ref[...].astype(F32) * _silu_tanh(z_ref[...].astype(F32))
    gw = D_INNER // N_SSM_GROUPS
    parts = []
    for g in range(N_SSM_GROUPS):
        yg = y[:, g * gw:(g + 1) * gw]
        ms = jnp.mean(yg * yg, axis=-1, keepdims=True)
        parts.append((yg * lax.rsqrt(ms + NORM_EPS) * nw_ref[:, g * gw:(g + 1) * gw]).astype(BF16))
    y_ssm = jnp.dot(jnp.concatenate(parts, axis=1), wso_ref[...], preferred_element_type=F32)
    y_att = jnp.dot(o_ref[...], wao_ref[...], preferred_element_type=F32)
    gates = 0.5 * jnp.tanh(0.5 * g_ref[...].astype(F32)) + 0.5
    mixed = (gates[:, :D_MODEL] * y_ssm + gates[:, D_MODEL:] * y_att).astype(BF16)
    out_ref[...] = x_ref[...] + jnp.dot(mixed, wo_ref[...], preferred_element_type=F32)


def _mix(y, z, o, gates, x, ssm_norm_w, w_ssm_out, w_att_out, w_o, tm=512):
    n = x.shape[0]
    row = lambda w: pl.BlockSpec((tm, w), lambda i: (i, 0))
    return pl.pallas_call(
        _mix_kernel,
        grid=(n // tm,),
        in_specs=[row(D_INNER), row(D_INNER), row(ATT_WIDTH), row(2 * D_MODEL), row(D_MODEL),
                  _resident((1, D_INNER)), _resident((D_INNER, D_MODEL)), _resident((ATT_WIDTH, D_MODEL)),
                  _resident((D_MODEL, D_MODEL))],
        out_specs=row(D_MODEL),
        out_shape=jax.ShapeDtypeStruct((n, D_MODEL), F32),
        compiler_params=pltpu.CompilerParams(dimension_semantics=("parallel",), vmem_limit_bytes=VMEM_LIMIT),
        name="mix",
    )(y, z, o, gates, x, ssm_norm_w, w_ssm_out, w_att_out, w_o)


def _ffn_kernel(x_ref, nw_ref, win_ref, wout_ref, fw_ref, out_ref, *, chunk):
    x = x_ref[...]
    ms = jnp.mean(x * x, axis=-1, keepdims=True)
    h = (x * lax.rsqrt(ms + NORM_EPS) * nw_ref[...]).astype(BF16)
    acc = x
    for c in range(0, D_FF, chunk):
        gate = jnp.dot(h, win_ref[:, c:c + chunk], preferred_element_type=F32)
        up = jnp.dot(h, win_ref[:, D_FF + c:D_FF + c + chunk], preferred_element_type=F32)
        hid = (_silu_tanh(gate) * up).astype(BF16)
        acc = acc + jnp.dot(hid, wout_ref[c:c + chunk, :], preferred_element_type=F32)
    ms2 = jnp.mean(acc * acc, axis=-1, keepdims=True)
    out_ref[...] = acc * lax.rsqrt(ms2 + NORM_EPS) * fw_ref[...]


def _ffn(x, norm_w, w_in, w_out, final_w, tm=1024, chunk=256):
    n = x.shape[0]
    tm = min(tm, n)
    row = pl.BlockSpec((tm, D_MODEL), lambda i: (i, 0))
    return pl.pallas_call(
        functools.partial(_ffn_kernel, chunk=chunk),
        grid=(n // tm,),
        in_specs=[row, _resident((1, D_MODEL)), _resident((D_MODEL, 2 * D_FF)), _resident((D_FF, D_MODEL)),
                  _resident((1, D_MODEL))],
        out_specs=row,
        out_shape=jax.ShapeDtypeStruct((n, D_MODEL), F32),
        compiler_params=pltpu.CompilerParams(dimension_semantics=("parallel",), vmem_limit_bytes=VMEM_LIMIT),
        name="ffn",
    )(x, norm_w, w_in, w_out, final_w)


def _pack_w_in(w_in):
    cuts = np.cumsum((D_INNER, CONV_DIM, N_SSM_HEADS, ATT_WIDTH, KV_WIDTH, KV_WIDTH, 2 * D_MODEL))[:-1]
    wz, wxbc, wdt, wq, wk, wv, wg = jnp.split(w_in, [int(c) for c in cuts], axis=-1)
    wdt = jnp.pad(wdt, ((0, 0), (0, DT_PAD - N_SSM_HEADS)))
    wq = wq * (ATT_HEAD_DIM ** -0.5 * math.log2(math.e))
    return jnp.concatenate([wz, wxbc, wq, wk, wv, wg, wdt], axis=-1).astype(BF16)


def kernel(x_prompt, x_sample, cache_k, cache_v, state_conv, state_ssm, page_table, norm_mix_w, w_in, conv_w, conv_b, dt_bias, a_log, d_skip, ssm_norm_w, w_ssm_out, lambda_q1, lambda_k1, lambda_q2, lambda_k2, subln_w, w_att_out, w_o, norm_ffn_w, w_ffn_in, w_ffn_out, norm_final_w):
    l = 0
    w_all = _pack_w_in(w_in[l])
    pad_heads = lambda t: jnp.pad(t.reshape(1, N_SSM_HEADS), ((0, 0), (0, DT_PAD - N_SSM_HEADS)))
    lam_vecs = jnp.stack([lambda_q1[l], lambda_k1[l], lambda_q2[l], lambda_k2[l]])
    params = dict(
        gain=norm_mix_w[l].reshape(1, D_MODEL), conv_w=conv_w[l], conv_b=conv_b[l].reshape(1, CONV_DIM),
        dt_bias=pad_heads(dt_bias[l]), a_log=pad_heads(a_log[l]),
        d_skip_x=jnp.repeat(d_skip[l], SSM_HEAD_DIM).reshape(1, D_INNER),
        ssm_norm_w=ssm_norm_w[l].reshape(1, D_INNER), w_ssm_out=w_ssm_out[l].astype(BF16),
        subln_w=subln_w[l].reshape(1, 2 * ATT_HEAD_DIM), w_att_out=w_att_out[l].astype(BF16),
        w_o=w_o[l].astype(BF16), norm_ffn_w=norm_ffn_w[l].reshape(1, D_MODEL),
        w_ffn_in=w_ffn_in[l].astype(BF16), w_ffn_out=w_ffn_out[l].astype(BF16),
        norm_final_w=norm_final_w.reshape(1, D_MODEL))

    def group(x, conv0, h0, attend):
        b, seq, _ = x.shape
        n = b * seq
        x2 = x.reshape(n, D_MODEL)
        z, xbc, q, k, v, gates, dt = _inproj(x2, params["gain"], w_all)
        y, conv_state, ssm_state = _ssd(
            xbc.reshape(b, seq, CONV_DIM), dt.reshape(b, seq, DT_PAD), conv0, h0.reshape(b, D_INNER, D_STATE),
            params["conv_w"], params["conv_b"], params["dt_bias"], params["a_log"], params["d_skip_x"])
        k3 = k.reshape(b, seq * N_KV_HEADS, HEAD_W)
        v3 = v.reshape(b, seq * N_KV_HEADS, HEAD_W)
        o = attend(q.reshape(b, seq, ATT_WIDTH), k3, v3)
        x1 = _mix(y.reshape(n, D_INNER), z, o.reshape(n, ATT_WIDTH), gates, x2, params["ssm_norm_w"],
                  params["w_ssm_out"], params["w_att_out"], params["w_o"])
        out = _ffn(x1, params["norm_ffn_w"], params["w_ffn_in"], params["w_ffn_out"], params["norm_final_w"])
        kv_shape = (1, b, seq, N_KV_HEADS, 2 * ATT_HEAD_DIM)
        return (out.reshape(b, seq, D_MODEL), k.reshape(kv_shape), v.reshape(kv_shape), conv_state[None],
                ssm_state.reshape(1, b, N_SSM_HEADS, SSM_HEAD_DIM, D_STATE))

    bp = x_prompt.shape[0]
    attend_p = lambda q, k, v: _attn_prompt(q, k, v, lam_vecs, params["subln_w"])
    yp, kp, vp, cp, hp = group(x_prompt, jnp.zeros((bp, D_CONV - 1, CONV_DIM), F32),
                               jnp.zeros((bp, N_SSM_HEADS, SSM_HEAD_DIM, D_STATE), F32), attend_p)
    n_pool = cache_k.shape[1]
    ck = cache_k[l].reshape(n_pool, PAGE_SIZE * N_KV_HEADS, HEAD_W)
    cv = cache_v[l].reshape(n_pool, PAGE_SIZE * N_KV_HEADS, HEAD_W)
    attend_s = lambda q, k, v: _attn_paged(q, k, v, ck, cv, page_table, lam_vecs, params["subln_w"])
    ys, ks, vs, cs, hs = group(x_sample, state_conv[l], state_ssm[l], attend_s)
    return (yp, ys, kp, vp, cp, hp, ks, vs, cs, hs)
```
